```python
import math
import jax
import jax.numpy as jnp
from jax import lax
import numpy as np

D_MODEL = 4096
BATCH = 32
SEQ = 256
DEPTH = 2
DEC_BATCH = 4
DEC_SEQ = 4096
PAST_LEN = 256

GRID_W = 64
HEAD_WIDTH = 128
DIFF_HEADS = D_MODEL // 4 // HEAD_WIDTH
DIFF_QK_DIM = HEAD_WIDTH // 2
DIFF_V_DIM = HEAD_WIDTH
GQA_HEADS = D_MODEL // 4 // HEAD_WIDTH
GQA_KV_HEADS = 2
GQA_HEAD_DIM = HEAD_WIDTH
SSD_WIDTH = D_MODEL // 2
SSD_HEAD_DIM = 64
SSD_HEADS = SSD_WIDTH // SSD_HEAD_DIM
SSD_GROUPS = 4
SSD_STATE = 128
SSD_CONV = 3
SSD_CHUNK = 128
QBLOCK = 128
ROPE_THETA = 10000.0
RMS_EPS = 1e-6
N_MOD = 6
N_EXPERTS = 64
EXPERT_FF = D_MODEL // 8
SHARED_FF = D_MODEL // 8
TOP_K = 8
N_EXPERT_GROUPS = 8
TOPK_GROUPS = 4
ROUTED_SCALE = 2.5
MOE_BLOCK = 128
DIFF_QK_COLS = DIFF_HEADS * 2 * DIFF_QK_DIM
DIFF_V_COLS = DIFF_HEADS * DIFF_V_DIM
GQA_Q_COLS = GQA_HEADS * GQA_HEAD_DIM
GQA_KV_COLS = GQA_KV_HEADS * GQA_HEAD_DIM
SSD_BC_COLS = SSD_GROUPS * SSD_STATE
SSD_XBC_COLS = SSD_WIDTH + 2 * SSD_BC_COLS
SSD_DT_COLS = 2 * SSD_HEADS
IN_SPLITS = (DIFF_QK_COLS, DIFF_QK_COLS, DIFF_V_COLS, GQA_Q_COLS, GQA_KV_COLS, GQA_KV_COLS, SSD_WIDTH, SSD_XBC_COLS, SSD_DT_COLS)
IN_DIM = 2 * DIFF_QK_COLS + DIFF_V_COLS + GQA_Q_COLS + 2 * GQA_KV_COLS + SSD_WIDTH + SSD_XBC_COLS + SSD_DT_COLS
MIX_OUT = DIFF_V_COLS + GQA_Q_COLS + SSD_WIDTH

kernel_name = 'hybrid_diffusion_prefix_trunk'


def rmsnorm(x, g):
    xf = x.astype(jnp.float32)
    y = xf * lax.rsqrt(jnp.mean(xf * xf, axis=-1, keepdims=True) + RMS_EPS)
    return (y * g.astype(jnp.float32)).astype(x.dtype)


def axial_rope(n_tokens, dim):
    rows = n_tokens // GRID_W
    row = jnp.repeat(jnp.arange(rows), GRID_W).astype(jnp.float32)
    col = (jnp.arange(rows * GRID_W) % GRID_W).astype(jnp.float32)
    n_freq = dim // 4
    inv = jnp.exp(-math.log(ROPE_THETA) * jnp.arange(n_freq, dtype=jnp.float32) / n_freq)
    ang = jnp.concatenate([row[:, None] * inv, col[:, None] * inv], axis=-1)
    return jnp.cos(ang), jnp.sin(ang)


def apply_rope(x, cos, sin):
    shape = (cos.shape[0],) + (1,) * (x.ndim - 3) + (cos.shape[1],)
    c = cos.reshape(shape).astype(x.dtype)
    s = sin.reshape(shape).astype(x.dtype)
    x1, x2 = jnp.split(x, 2, axis=-1)
    return jnp.concatenate([x1 * c - x2 * s, x1 * s + x2 * c], axis=-1)


def sweep_query_blocks(fn, q):
    b, n = q.shape[:2]
    nb = n // QBLOCK
    qb = jnp.moveaxis(q.reshape((b, nb, QBLOCK) + q.shape[2:]), 1, 0)
    out = jnp.moveaxis(lax.map(fn, qb), 0, 1)
    return out.reshape((b, n) + out.shape[3:])


def diff_attention(q, k, v, lam):
    scale = DIFF_QK_DIM ** -0.5
    def block(qb):
        s = jnp.einsum('bqhmd,bkhmd->bhmqk', qb, k).astype(jnp.float32) * scale
        p = jax.nn.softmax(s, axis=-1)
        p_diff = p[:, :, 0] - lam * p[:, :, 1]
        return jnp.einsum('bhqk,bkhd->bqhd', p_diff.astype(v.dtype), v)
    return sweep_query_blocks(block, q)


def gqa_attention(q, k, v):
    scale = GQA_HEAD_DIM ** -0.5
    def block(qb):
        s = jnp.einsum('bqgrd,bkgd->bgrqk', qb, k).astype(jnp.float32) * scale
        p = jax.nn.softmax(s, axis=-1).astype(v.dtype)
        return jnp.einsum('bgrqk,bkgd->bqgrd', p, v)
    return sweep_query_blocks(block, q)


def depthwise_conv(u, w, bias):
    pad = SSD_CONV // 2
    y = lax.conv_general_dilated(u, w[:, None, :].astype(u.dtype), window_strides=(1,), padding=[(pad, pad)],
                                 dimension_numbers=('NWC', 'WIO', 'NWC'), feature_group_count=u.shape[-1])
    return y + bias


def ssd_scan(x, dt, a, bm, cm, h0):
    b, n, g, r, p = x.shape
    nc = n // SSD_CHUNK
    x = x.reshape(b, nc, SSD_CHUNK, g, r, p)
    dt = dt.reshape(b, nc, SSD_CHUNK, g, r)
    bm = bm.reshape(b, nc, SSD_CHUNK, g, -1)
    cm = cm.reshape(b, nc, SSD_CHUNK, g, -1)
    a_cum = jnp.cumsum(dt * a, axis=2)
    seg = a_cum[:, :, :, None] - a_cum[:, :, None]
    causal = jnp.tril(jnp.ones((SSD_CHUNK, SSD_CHUNK), bool))[:, :, None, None]
    decay = jnp.exp(jnp.where(causal, seg, -jnp.inf))
    cb = jnp.einsum('bcqgn,bckgn->bcqkg', cm, bm)
    w = cb[..., None] * decay * dt[:, :, None]
    y_diag = jnp.einsum('bcqkgr,bckgrp->bcqgrp', w, x)
    to_end = jnp.exp(a_cum[:, :, -1:] - a_cum) * dt
    chunk_states = jnp.einsum('bckgn,bckgr,bckgrp->bcgrpn', bm, to_end, x)
    chunk_decay = jnp.exp(a_cum[:, :, -1])
    def step(h, inp):
        st, dec = inp
        return h * dec[..., None, None] + st, h
    h_last, h_in = lax.scan(step, h0, (jnp.moveaxis(chunk_states, 1, 0), jnp.moveaxis(chunk_decay, 1, 0)))
    h_in = jnp.moveaxis(h_in, 0, 1)
    y_off = jnp.einsum('bcqgn,bcgrpn,bcqgr->bcqgrp', cm, h_in, jnp.exp(a_cum))
    return (y_diag + y_off).reshape(b, n, g, r, p), h_last


def ssd_mixer(z, xbc, dt_raw, conv_w, conv_b, dt_bias, a_log, d_skip, norm_w, h0):
    b, n, _ = z.shape
    r = SSD_HEADS // SSD_GROUPS
    xbc = jax.nn.silu(depthwise_conv(xbc, conv_w, conv_b))
    xs, bm, cm = jnp.split(xbc, [SSD_WIDTH, SSD_WIDTH + SSD_BC_COLS], axis=-1)
    xs = xs.reshape(b, n, SSD_GROUPS, r, SSD_HEAD_DIM).astype(jnp.float32)
    bm = bm.reshape(b, n, SSD_GROUPS, SSD_STATE).astype(jnp.float32)
    cm = cm.reshape(b, n, SSD_GROUPS, SSD_STATE).astype(jnp.float32)
    dt = jax.nn.softplus(dt_raw.astype(jnp.float32).reshape(b, n, 2, SSD_HEADS) + dt_bias.astype(jnp.float32))
    dt = dt.reshape(b, n, 2, SSD_GROUPS, r)
    a = -jnp.exp(a_log.astype(jnp.float32)).reshape(2, SSD_GROUPS, r)
    h0 = h0.astype(jnp.float32).reshape(b, 2, SSD_GROUPS, r, SSD_HEAD_DIM, SSD_STATE)
    flip = lambda t: jnp.flip(t, axis=1)
    y_f, h_f = ssd_scan(xs, dt[:, :, 0], a[0], bm, cm, h0[:, 0])
    y_b, h_b = ssd_scan(flip(xs), flip(dt[:, :, 1]), a[1], flip(bm), flip(cm), h0[:, 1])
    y = y_f + flip(y_b) + d_skip.astype(jnp.float32).reshape(SSD_GROUPS, r)[:, :, None] * xs
    y = y.reshape(b, n, SSD_WIDTH) * jax.nn.silu(z.astype(jnp.float32))
    y = rmsnorm(y.reshape(b, n, SSD_GROUPS, SSD_WIDTH // SSD_GROUPS), norm_w.reshape(SSD_GROUPS, -1))
    h_last = jnp.stack([h_f, h_b], axis=1).reshape(b, 2, SSD_HEADS, SSD_HEAD_DIM, SSD_STATE)
    return y.reshape(b, n, SSD_WIDTH).astype(z.dtype), h_last


def route(h, router_w, router_bias):
    t = h.shape[0]
    per_group = N_EXPERTS // N_EXPERT_GROUPS
    scores = jax.nn.sigmoid((h @ router_w).astype(jnp.float32))
    sel = scores + router_bias.astype(jnp.float32)
    group_score = lax.top_k(sel.reshape(t, N_EXPERT_GROUPS, per_group), 2)[0].sum(-1)
    _, top_groups = lax.top_k(group_score, TOPK_GROUPS)
    group_mask = jax.nn.one_hot(top_groups, N_EXPERT_GROUPS, dtype=jnp.float32).sum(1) > 0
    expert_mask = jnp.repeat(group_mask, per_group, axis=1)
    _, idx = lax.top_k(jnp.where(expert_mask, sel, -jnp.inf), TOP_K)
    w = jnp.take_along_axis(scores, idx, axis=1)
    return idx, w / jnp.sum(w, axis=-1, keepdims=True) * ROUTED_SCALE


def swiglu(h, w_gate, w_up, w_down):
    return (jax.nn.silu(h @ w_gate) * (h @ w_up)) @ w_down


def routed_experts(h, idx, wts, w_gate, w_up, w_down):
    t, d = h.shape
    tk = t * TOP_K
    n_blocks = -(-tk // MOE_BLOCK) + N_EXPERTS
    m = n_blocks * MOE_BLOCK
    flat_e = idx.reshape(tk)
    order = jnp.argsort(flat_e)
    e_sorted = flat_e[order]
    tok_sorted = (order // TOP_K).astype(jnp.int32)
    w_sorted = wts.reshape(tk)[order]
    counts = jnp.bincount(flat_e, length=N_EXPERTS)
    padded = (counts + MOE_BLOCK - 1) // MOE_BLOCK * MOE_BLOCK
    pad_end = jnp.cumsum(padded)
    pad_start = pad_end - padded
    start = jnp.cumsum(counts) - counts
    dest = pad_start[e_sorted] + jnp.arange(tk) - start[e_sorted]
    row_tok = jnp.full((m,), t, jnp.int32).at[dest].set(tok_sorted)
    row_w = jnp.zeros((m,), h.dtype).at[dest].set(w_sorted)
    blk_e = jnp.minimum(jnp.searchsorted(pad_end, jnp.arange(n_blocks) * MOE_BLOCK, side='right'), N_EXPERTS - 1)
    h_pad = jnp.concatenate([h, jnp.zeros((1, d), h.dtype)], axis=0)
    def body(acc, blk):
        tok_b, w_b, e = blk
        y = swiglu(h_pad[tok_b], w_gate[e], w_up[e], w_down[e])
        return acc.at[tok_b].add(y * w_b[:, None]), None
    acc, _ = lax.scan(body, jnp.zeros((t + 1, d), h.dtype),
                      (row_tok.reshape(n_blocks, MOE_BLOCK), row_w.reshape(n_blocks, MOE_BLOCK), blk_e))
    return acc[:t]


def moe_ffn(h, router_w, router_bias, w_gate, w_up, w_down, s_gate, s_up, s_down):
    idx, wts = route(h, router_w, router_bias)
    return swiglu(h, s_gate, s_up, s_down) + routed_experts(h, idx, wts.astype(h.dtype), w_gate, w_up, w_down)


def trunk_layer(x, cond, li, rope, cache, weights):
    (w_ada, b_ada, norm_g, w_in, w_out, diff_lambda, diff_subnorm, gqa_qk_norm,
     ssd_conv_w, ssd_conv_b, ssd_dt_bias, ssd_a_log, ssd_d, ssd_norm,
     router_w, router_bias, exp_w_gate, exp_w_up, exp_w_down,
     sh_w_gate, sh_w_up, sh_w_down) = weights
    b, n, _ = x.shape
    mod = (jax.nn.silu(cond) @ w_ada[li] + b_ada[li]).reshape(cond.shape[0], 1, N_MOD, D_MODEL)
    shift_m, scale_m, gate_m, shift_f, scale_f, gate_f = [mod[:, :, i] for i in range(N_MOD)]
    g = norm_g[li]

    h = rmsnorm(x, g[0]) * (1 + scale_m) + shift_m
    split_points = np.cumsum(IN_SPLITS)[:-1].tolist()
    dq, dk, dv, gq, gk, gv, z, xbc, dt_raw = jnp.split(h @ w_in[li], split_points, axis=-1)
    r_gqa = GQA_HEADS // GQA_KV_HEADS
    dq = dq.reshape(b, n, DIFF_HEADS, 2, DIFF_QK_DIM)
    dk = dk.reshape(b, n, DIFF_HEADS, 2, DIFF_QK_DIM)
    dv = dv.reshape(b, n, DIFF_HEADS, DIFF_V_DIM)
    gq = rmsnorm(gq.reshape(b, n, GQA_KV_HEADS, r_gqa, GQA_HEAD_DIM), gqa_qk_norm[li, 0])
    gk = rmsnorm(gk.reshape(b, n, GQA_KV_HEADS, GQA_HEAD_DIM), gqa_qk_norm[li, 1])
    gv = gv.reshape(b, n, GQA_KV_HEADS, GQA_HEAD_DIM)

    if cache is None:
        dk_all, dv_all, gk_all, gv_all = dk, dv, gk, gv
        h0 = jnp.zeros((b, 2, SSD_HEADS, SSD_HEAD_DIM, SSD_STATE), jnp.float32)
    else:
        (cos_d, sin_d), (cos_g, sin_g) = rope
        ctx_dk, ctx_dv, ctx_gk, ctx_gv, h0 = cache
        dq = apply_rope(dq, cos_d, sin_d)
        dk = apply_rope(dk, cos_d, sin_d)
        gq = apply_rope(gq, cos_g, sin_g)
        gk = apply_rope(gk, cos_g, sin_g)
        dk_all = jnp.concatenate([ctx_dk, dk], axis=1)
        dv_all = jnp.concatenate([ctx_dv, dv], axis=1)
        gk_all = jnp.concatenate([ctx_gk, gk], axis=1)
        gv_all = jnp.concatenate([ctx_gv, gv], axis=1)

    lam_init = 0.8 - 0.6 * math.exp(-0.3 * li)
    lq1, lk1, lq2, lk2 = diff_lambda[li].astype(jnp.float32)
    lam = jnp.exp(jnp.sum(lq1 * lk1)) - jnp.exp(jnp.sum(lq2 * lk2)) + lam_init
    d_out = rmsnorm(diff_attention(dq, dk_all, dv_all, lam), diff_subnorm[li]) * (1.0 - lam_init)
    g_out = gqa_attention(gq, gk_all, gv_all)
    s_out, h_last = ssd_mixer(z, xbc, dt_raw, ssd_conv_w[li], ssd_conv_b[li], ssd_dt_bias[li],
                              ssd_a_log[li], ssd_d[li], ssd_norm[li], h0)
    mixed = jnp.concatenate([d_out.reshape(b, n, -1), g_out.reshape(b, n, -1), s_out], axis=-1)
    x = x + gate_m * rmsnorm(mixed @ w_out[li], g[1])

    h = rmsnorm(x, g[2]) * (1 + scale_f) + shift_f
    f = moe_ffn(h.reshape(b * n, D_MODEL), router_w[li], router_bias[li], exp_w_gate[li], exp_w_up[li],
                exp_w_down[li], sh_w_gate[li], sh_w_up[li], sh_w_down[li]).reshape(b, n, D_MODEL)
    x = x + gate_f * rmsnorm(f, g[3])
    if cache is None:
        return x, (dk, dv, gk, gv, h_last.astype(x.dtype))
    return x, None


def setup_inputs(seed: int = 0) -> dict:
    key = jax.random.key(seed)
    k = jax.random.split(key, 32)
    f32 = jnp.float32
    nrm = lambda kk, shape, s: jax.random.normal(kk, shape, f32) * s
    dt0 = jnp.exp(jax.random.uniform(k[17], (DEPTH, 2, SSD_HEADS), f32, math.log(1e-3), math.log(1e-1)))
    return {
        'x_prompt': nrm(k[0], (BATCH, SEQ, D_MODEL), 1.0),
        'x_sample': nrm(k[1], (DEC_BATCH, DEC_SEQ, D_MODEL), 1.0),
        'cache_diff_k': nrm(k[2], (DEC_BATCH, DEPTH, PAST_LEN, DIFF_HEADS, 2, DIFF_QK_DIM), 1.0),
        'cache_diff_v': nrm(k[3], (DEC_BATCH, DEPTH, PAST_LEN, DIFF_HEADS, DIFF_V_DIM), 1.0),
        'cache_gqa_k': nrm(k[4], (DEC_BATCH, DEPTH, PAST_LEN, GQA_KV_HEADS, GQA_HEAD_DIM), 1.0),
        'cache_gqa_v': nrm(k[5], (DEC_BATCH, DEPTH, PAST_LEN, GQA_KV_HEADS, GQA_HEAD_DIM), 1.0),
        'state_ssd': nrm(k[6], (DEC_BATCH, DEPTH, 2, SSD_HEADS, SSD_HEAD_DIM, SSD_STATE), 0.1),
        'c': nrm(k[7], (DEC_BATCH, D_MODEL), 1.0),
        'c_ctx': nrm(k[8], (D_MODEL,), 1.0),
        'w_ada': nrm(k[9], (DEPTH, D_MODEL, N_MOD * D_MODEL), 0.5 * D_MODEL ** -0.5),
        'b_ada': nrm(k[10], (DEPTH, N_MOD * D_MODEL), 0.02),
        'norm_g': 1.0 + nrm(k[11], (DEPTH, 4, D_MODEL), 0.02),
        'w_in': nrm(k[12], (DEPTH, D_MODEL, IN_DIM), D_MODEL ** -0.5),
        'w_out': nrm(k[13], (DEPTH, MIX_OUT, D_MODEL), MIX_OUT ** -0.5),
        'diff_lambda': nrm(k[14], (DEPTH, 4, DIFF_QK_DIM), 0.1),
        'diff_subnorm': 1.0 + nrm(k[15], (DEPTH, DIFF_V_DIM), 0.02),
        'gqa_qk_norm': 1.0 + nrm(k[16], (DEPTH, 2, GQA_HEAD_DIM), 0.02),
        'ssd_conv_w': nrm(k[18], (DEPTH, SSD_CONV, SSD_XBC_COLS), SSD_CONV ** -0.5),
        'ssd_conv_b': nrm(k[19], (DEPTH, SSD_XBC_COLS), 0.02),
        'ssd_dt_bias': dt0 + jnp.log(-jnp.expm1(-dt0)),
        'ssd_a_log': jnp.log(jax.random.uniform(k[20], (DEPTH, 2, SSD_HEADS), f32, 1.0, 16.0)),
        'ssd_d': 1.0 + nrm(k[21], (DEPTH, SSD_HEADS), 0.02),
        'ssd_norm': 1.0 + nrm(k[22], (DEPTH, SSD_WIDTH), 0.02),
        'router_w': nrm(k[23], (DEPTH, D_MODEL, N_EXPERTS), D_MODEL ** -0.5),
        'router_bias': nrm(k[24], (DEPTH, N_EXPERTS), 0.01),
        'exp_w_gate': nrm(k[25], (DEPTH, N_EXPERTS, D_MODEL, EXPERT_FF), D_MODEL ** -0.5),
        'exp_w_up': nrm(k[26], (DEPTH, N_EXPERTS, D_MODEL, EXPERT_FF), D_MODEL ** -0.5),
        'exp_w_down': nrm(k[27], (DEPTH, N_EXPERTS, EXPERT_FF, D_MODEL), EXPERT_FF ** -0.5),
        'sh_w_gate': nrm(k[28], (DEPTH, D_MODEL, SHARED_FF), D_MODEL ** -0.5),
        'sh_w_up': nrm(k[29], (DEPTH, D_MODEL, SHARED_FF), D_MODEL ** -0.5),
        'sh_w_down': nrm(k[30], (DEPTH, SHARED_FF, D_MODEL), SHARED_FF ** -0.5),
    }


def reference(x_prompt, x_sample, cache_diff_k, cache_diff_v, cache_gqa_k, cache_gqa_v, state_ssd, c, c_ctx,
              w_ada, b_ada, norm_g, w_in, w_out, diff_lambda, diff_subnorm, gqa_qk_norm,
              ssd_conv_w, ssd_conv_b, ssd_dt_bias, ssd_a_log, ssd_d, ssd_norm,
              router_w, router_bias, exp_w_gate, exp_w_up, exp_w_down, sh_w_gate, sh_w_up, sh_w_down):
    weights = (w_ada, b_ada, norm_g, w_in, w_out, diff_lambda, diff_subnorm, gqa_qk_norm,
               ssd_conv_w, ssd_conv_b, ssd_dt_bias, ssd_a_log, ssd_d, ssd_norm,
               router_w, router_bias, exp_w_gate, exp_w_up, exp_w_down, sh_w_gate, sh_w_up, sh_w_down)

    y_prompt = x_prompt
    ctx_per_layer = []
    for li in range(DEPTH):
        y_prompt, ctx_t = trunk_layer(y_prompt, c_ctx[None, :], li, None, None, weights)
        ctx_per_layer.append(ctx_t)
    new_diff_k = jnp.stack([t[0] for t in ctx_per_layer], axis=1)
    new_diff_v = jnp.stack([t[1] for t in ctx_per_layer], axis=1)
    new_gqa_k = jnp.stack([t[2] for t in ctx_per_layer], axis=1)
    new_gqa_v = jnp.stack([t[3] for t in ctx_per_layer], axis=1)
    new_state_ssd = jnp.stack([t[4] for t in ctx_per_layer], axis=1)

    n_lat = x_sample.shape[1]
    rope = (axial_rope(n_lat, DIFF_QK_DIM), axial_rope(n_lat, GQA_HEAD_DIM))
    y_sample = x_sample
    for li in range(DEPTH):
        cache = (cache_diff_k[:, li], cache_diff_v[:, li], cache_gqa_k[:, li], cache_gqa_v[:, li], state_ssd[:, li])
        y_sample, _ = trunk_layer(y_sample, c, li, rope, cache, weights)

    return (y_prompt, y_sample, new_diff_k, new_diff_v, new_gqa_k, new_gqa_v, new_state_ssd)
```

```python
import functools
import math

import jax
import jax.numpy as jnp
from jax import lax
from jax.experimental import pallas as pl
from jax.experimental.pallas import tpu as pltpu

F32 = jnp.float32
BF16 = jnp.bfloat16
I32 = jnp.int32

LANES = 128
HEAD_WIDTH = 128
GRID_W = 64
GQA_KV_HEADS = 2
SSD_HEAD_DIM = 64
SSD_GROUPS = 4
SSD_STATE = 128
SSD_CHUNK = 128
ROPE_THETA = 10000.0
RMS_EPS = 1e-6
N_MOD = 6
N_EXPERTS = 64
TOP_K = 8
N_EXPERT_GROUPS = 8
TOPK_GROUPS = 4
ROUTED_SCALE = 2.5
EXPERT_ROWS = 256
VMEM_LIMIT = 56 * 1024 * 1024
NEG_INF = float("-inf")


def _cparams(sem):
    return pltpu.CompilerParams(dimension_semantics=sem, vmem_limit_bytes=VMEM_LIMIT)


def _pick(n, cands):
    for c in cands:
        if n % c == 0:
            return c
    raise ValueError(f"no tile for {n} in {cands}")


def _silu(x):
    return x * (1.0 / (1.0 + jnp.exp(-x)))


def _rms(x, g):
    return x * lax.rsqrt(jnp.mean(x * x, axis=-1, keepdims=True) + RMS_EPS) * g


def _nt_dot(a, b):
    return lax.dot_general(a, b, (((1,), (1,)), ((), ())), preferred_element_type=F32)


def _tn_dot(a, b):
    return lax.dot_general(a, b, (((0,), (0,)), ((), ())), preferred_element_type=F32)


def _slab_load(ref, n, slab, lead=()):
    return jnp.concatenate([ref[lead + (pl.ds(a, n, stride=slab), slice(None))] for a in range(slab)], axis=1)


def _slab_store(ref, val, slab):
    n = val.shape[0]
    for a in range(slab):
        ref[pl.ds(a, n, stride=slab), :] = val[:, a * LANES:(a + 1) * LANES]


def _ada_kernel(cond_ref, w_ref, b_ref, o_ref):
    s = _silu(cond_ref[...]).astype(BF16)
    o_ref[0] = jnp.dot(s, w_ref[0].astype(BF16), preferred_element_type=F32) + b_ref[0]


def _ada(cond, w_ada, b_ada):
    depth, d, n = w_ada.shape
    tn = _pick(n, (512, 256, 128))
    return pl.pallas_call(
        _ada_kernel,
        grid=(depth, n // tn),
        in_specs=[
            pl.BlockSpec((8, d), lambda l, j: (0, 0)),
            pl.BlockSpec((1, d, tn), lambda l, j: (l, 0, j)),
            pl.BlockSpec((1, 1, tn), lambda l, j: (l, 0, j)),
        ],
        out_specs=pl.BlockSpec((1, 8, tn), lambda l, j: (l, 0, j)),
        out_shape=jax.ShapeDtypeStruct((depth, 8, n), F32),
        compiler_params=_cparams(("arbitrary", "arbitrary")),
        name="ada_mod",
    )(cond, w_ada, b_ada.reshape(depth, 1, n))


class _Tokens:
    def __init__(self, n_ctx, n_lat_seq):
        self.n_ctx = n_ctx
        self.n_lat_seq = n_lat_seq

    def mod_row(self, i, tm):
        assert self.n_ctx % tm == 0 and self.n_lat_seq % tm == 0
        ctx_tiles = self.n_ctx // tm
        per_b = self.n_lat_seq // tm
        return jnp.where(i < ctx_tiles, 0, 1 + (i - ctx_tiles) // per_b)


def _inproj_kernel(x_ref, mod_ref, g_ref, w_ref, wdt_ref, o_ref, odt_ref, h_scr):
    @pl.when(pl.program_id(1) == 0)
    def _():
        h = _rms(x_ref[...], g_ref[...]) * (1.0 + mod_ref[1:2, :]) + mod_ref[0:1, :]
        hb = h.astype(BF16)
        h_scr[...] = hb
        odt_ref[...] = jnp.dot(hb, wdt_ref[...], preferred_element_type=F32)

    o_ref[...] = jnp.dot(h_scr[...], w_ref[...], preferred_element_type=F32)


def _inproj(x, mod_l, g0, w_main, w_dt, toks):
    t, d = x.shape
    n = w_main.shape[1]
    tm = _pick(t, (512, 256, 128))
    tm = min(tm, _pick(toks.n_ctx, (512, 256, 128)), _pick(toks.n_lat_seq, (512, 256, 128)))
    tn = _pick(n, (512, 256, 128))
    ndt = w_dt.shape[1]
    return pl.pallas_call(
        _inproj_kernel,
        grid=(t // tm, n // tn),
        in_specs=[
            pl.BlockSpec((tm, d), lambda i, j: (i, 0)),
            pl.BlockSpec((None, N_MOD, d), lambda i, j: (toks.mod_row(i, tm), 0, 0)),
            pl.BlockSpec((1, d), lambda i, j: (0, 0)),
            pl.BlockSpec((d, tn), lambda i, j: (0, j)),
            pl.BlockSpec((d, ndt), lambda i, j: (0, 0)),
        ],
        out_specs=[
            pl.BlockSpec((tm, tn), lambda i, j: (i, j)),
            pl.BlockSpec((tm, ndt), lambda i, j: (i, 0)),
        ],
        out_shape=[jax.ShapeDtypeStruct((t, n), F32), jax.ShapeDtypeStruct((t, ndt), F32)],
        scratch_shapes=[pltpu.VMEM((tm, d), BF16)],
        compiler_params=_cparams(("arbitrary", "arbitrary")),
        name="in_proj",
    )(x, mod_l, g0, w_main, w_dt)


def _rope128(x, cos, sin, half):
    if half == 64:
        partner = pltpu.roll(x, 64, 1)
    else:
        lane = lax.broadcasted_iota(I32, x.shape, 1)
        partner = jnp.where((lane & (2 * half - 1)) < half, pltpu.roll(x, LANES - half, 1), pltpu.roll(x, half, 1))
    return x * cos + partner * sin


def _prep_kernel(*refs, n_dh, n_gh, n_kv, rope):
    if rope:
        p_ref, qkn_ref, cd_ref, sd_ref, cg_ref, sg_ref, qd_ref, kd_ref, vd_ref, qg_ref, kg_ref, vg_ref = refs
    else:
        p_ref, qkn_ref, qd_ref, kd_ref, vd_ref, qg_ref, kg_ref, vg_ref, kgf_ref = refs
    dw = n_dh * HEAD_WIDTH
    gw = n_gh * HEAD_WIDTH
    kw = n_kv * HEAD_WIDTH
    d_scale = (HEAD_WIDTH // 2) ** -0.5
    g_scale = HEAD_WIDTH ** -0.5
    for h in range(n_dh):
        sl = slice(h * HEAD_WIDTH, (h + 1) * HEAD_WIDTH)
        q = p_ref[:, sl]
        k = p_ref[:, dw + h * HEAD_WIDTH: dw + (h + 1) * HEAD_WIDTH]
        if rope:
            q = _rope128(q, cd_ref[...], sd_ref[...], HEAD_WIDTH // 4)
            k = _rope128(k, cd_ref[...], sd_ref[...], HEAD_WIDTH // 4)
        qd_ref[:, sl] = (q * d_scale).astype(BF16)
        kd_ref[:, sl] = k.astype(BF16)
    vd_ref[...] = p_ref[:, 2 * dw: 3 * dw].astype(BF16)
    off = 3 * dw
    for h in range(n_gh):
        sl = slice(h * HEAD_WIDTH, (h + 1) * HEAD_WIDTH)
        q = _rms(p_ref[:, off + h * HEAD_WIDTH: off + (h + 1) * HEAD_WIDTH], qkn_ref[0:1, :])
        if rope:
            q = _rope128(q, cg_ref[...], sg_ref[...], HEAD_WIDTH // 2)
        qg_ref[:, sl] = (q * g_scale).astype(BF16)
    off += gw
    for h in range(n_kv):
        sl = slice(h * HEAD_WIDTH, (h + 1) * HEAD_WIDTH)
        k = _rms(p_ref[:, off + h * HEAD_WIDTH: off + (h + 1) * HEAD_WIDTH], qkn_ref[1:2, :])
        if rope:
            k = _rope128(k, cg_ref[...], sg_ref[...], HEAD_WIDTH // 2)
        else:
            kgf_ref[:, sl] = k
        kg_ref[:, sl] = k.astype(BF16)
    off += kw
    vg_ref[...] = p_ref[:, off: off + kw].astype(BF16)


def _prep(proj, qk_norm, row0, n_rows, seq, n_dh, n_gh, n_kv, rope_tabs):
    dw, gw, kw = n_dh * HEAD_WIDTH, n_gh * HEAD_WIDTH, n_kv * HEAD_WIDTH
    width = 3 * dw + gw + 2 * kw
    tn = _pick(seq, (256, 128))
    assert row0 % tn == 0
    r0 = row0 // tn
    rope = rope_tabs is not None
    in_specs = [
        pl.BlockSpec((tn, width), lambda i: (r0 + i, 0)),
        pl.BlockSpec((2, HEAD_WIDTH), lambda i: (0, 0)),
    ]
    args = [proj, qk_norm]
    if rope:
        per_seq = seq // tn
        for tab in rope_tabs:
            in_specs.append(pl.BlockSpec((tn, HEAD_WIDTH), lambda i: (i % per_seq, 0)))
            args.append(tab)
    widths = [dw, dw, dw, gw, kw, kw]
    out_specs = [pl.BlockSpec((tn, w), lambda i: (i, 0)) for w in widths]
    out_shape = [jax.ShapeDtypeStruct((n_rows, w), BF16) for w in widths]
    if not rope:
        out_specs.append(pl.BlockSpec((tn, kw), lambda i: (i, 0)))
        out_shape.append(jax.ShapeDtypeStruct((n_rows, kw), F32))
    return pl.pallas_call(
        functools.partial(_prep_kernel, n_dh=n_dh, n_gh=n_gh, n_kv=n_kv, rope=rope),
        grid=(n_rows // tn,),
        in_specs=in_specs,
        out_specs=out_specs,
        out_shape=out_shape,
        compiler_params=_cparams(("arbitrary",)),
        name="attn_prep_rope" if rope else "attn_prep",
    )(*args)


def _softmax_step(state, s, v):
    m, l, acc = state
    m_new = jnp.maximum(m, jnp.max(s, axis=-1, keepdims=True))
    alpha = jnp.exp(m - m_new)
    p = jnp.exp(s - m_new)
    l = alpha * l + jnp.sum(p, axis=-1, keepdims=True)
    acc = alpha * acc + jnp.dot(p.astype(BF16), v, preferred_element_type=F32)
    return m_new, l, acc


def _attn_kernel(*refs, seg_lens, diff, lam_init):
    n_seg = len(seg_lens)
    q_ref = refs[0]
    kv_refs = refs[1: 1 + 2 * n_seg]
    if diff:
        lam_ref, sub_ref, o_ref = refs[1 + 2 * n_seg:]
    else:
        (o_ref,) = refs[1 + 2 * n_seg:]
    q = q_ref[...]
    tq = q.shape[0]
    if diff:
        lane = lax.broadcasted_iota(I32, q.shape, 1)
        qs = [jnp.where(lane < HEAD_WIDTH // 2, q, jnp.zeros_like(q)),
              jnp.where(lane < HEAD_WIDTH // 2, jnp.zeros_like(q), q)]
    else:
        qs = [q]
    init = (jnp.full((tq, 1), NEG_INF, F32), jnp.zeros((tq, 1), F32), jnp.zeros((tq, HEAD_WIDTH), F32))
    states = [init for _ in qs]

    for si, t_len in enumerate(seg_lens):
        k_ref, v_ref = kv_refs[2 * si], kv_refs[2 * si + 1]
        tk = _pick(t_len, (512, 256, 128))

        def chunk(c, sts, k_ref=k_ref, v_ref=v_ref, tk=tk):
            start = pl.multiple_of(c * tk, tk)
            k = k_ref[pl.ds(start, tk), :].astype(BF16)
            v = v_ref[pl.ds(start, tk), :].astype(BF16)
            return tuple(_softmax_step(st, _nt_dot(qm, k), v) for st, qm in zip(sts, qs))

        if t_len == tk:
            states = list(chunk(0, tuple(states)))
        else:
            states = list(lax.fori_loop(0, t_len // tk, chunk, tuple(states)))

    outs = [acc / l for (_, l, acc) in states]
    if diff:
        lp = lam_ref[...]
        lam = (jnp.exp(jnp.sum(lp[0:1] * lp[1:2], axis=-1, keepdims=True))
               - jnp.exp(jnp.sum(lp[2:3] * lp[3:4], axis=-1, keepdims=True)) + lam_init)
        o = outs[0] - lam * outs[1]
        o = _rms(o, sub_ref[...]) * (1.0 - lam_init)
    else:
        o = outs[0]
    o_ref[...] = o.astype(o_ref.dtype)


def _attention(q, kvs, n_heads, kv_group, batch, seq, diff=None):
    tq = _pick(seq, (256, 128))
    per_seq = seq // tq
    in_specs = [pl.BlockSpec((tq, HEAD_WIDTH), lambda b, h, i: (b * per_seq + i, h))]
    args = [q]
    seg_lens = []
    for k, v, t_len in kvs:
        for a in (k, v):
            in_specs.append(pl.BlockSpec((None, t_len, HEAD_WIDTH), lambda b, h, i: (b, 0, h // kv_group)))
            args.append(a)
        seg_lens.append(t_len)
    lam_init = 0.0
    if diff is not None:
        lam_params, subnorm, lam_init = diff
        in_specs.append(pl.BlockSpec(lam_params.shape, lambda b, h, i: (0, 0)))
        in_specs.append(pl.BlockSpec((1, HEAD_WIDTH), lambda b, h, i: (0, 0)))
        args += [lam_params, subnorm]
    return pl.pallas_call(
        functools.partial(_attn_kernel, seg_lens=tuple(seg_lens), diff=diff is not None, lam_init=lam_init),
        grid=(batch, n_heads, per_seq),
        in_specs=in_specs,
        out_specs=pl.BlockSpec((tq, HEAD_WIDTH), lambda b, h, i: (b * per_seq + i, h)),
        out_shape=jax.ShapeDtypeStruct((batch * seq, n_heads * HEAD_WIDTH), BF16),
        compiler_params=_cparams(("arbitrary", "arbitrary", "arbitrary")),
        name="diff_attn" if diff is not None else "gqa_attn",
    )(*args)


def _conv_kernel(u_ref, w_ref, b_ref, o_ref):
    u = u_ref[...]
    n = u.shape[0]
    row = lax.broadcasted_iota(I32, u.shape, 0)
    prev = jnp.where(row == 0, 0.0, pltpu.roll(u, 1, 0))
    nxt = jnp.where(row == n - 1, 0.0, pltpu.roll(u, n - 1, 0))
    y = w_ref[0:1, :] * prev + w_ref[1:2, :] * u + w_ref[2:3, :] * nxt + b_ref[...]
    o_ref[...] = _silu(y)


def _conv_silu(proj, col0, width, conv_w, conv_b, row0, batch, seq):
    tc = 256
    assert col0 % tc == 0 and width % tc == 0 and row0 % seq == 0
    c0, r0 = col0 // tc, row0 // seq
    return pl.pallas_call(
        _conv_kernel,
        grid=(batch, width // tc),
        in_specs=[
            pl.BlockSpec((seq, tc), lambda b, j: (r0 + b, c0 + j)),
            pl.BlockSpec((conv_w.shape[0], tc), lambda b, j: (0, j)),
            pl.BlockSpec((1, tc), lambda b, j: (0, j)),
        ],
        out_specs=pl.BlockSpec((seq, tc), lambda b, j: (b, j)),
        out_shape=jax.ShapeDtypeStruct((batch * seq, width), F32),
        compiler_params=_cparams(("arbitrary", "arbitrary")),
        name="ssd_conv",
    )(proj, conv_w, conv_b)


def _split3(x):
    hi = x.astype(BF16)
    r = x - hi.astype(F32)
    mid = r.astype(BF16)
    lo = (r - mid.astype(F32)).astype(BF16)
    return hi, mid, lo


def _ssd_kernel(*refs, n_heads, heads_per_group, has_h0, want_state):
    it = iter(refs)
    xs_ref, bm_ref, cm_ref, dt_ref, dtb_ref, alog_ref = (next(it) for _ in range(6))
    h0_ref = next(it) if has_h0 else None
    y_ref = next(it)
    hl_ref = next(it) if want_state else None
    state = next(it)

    fwd = pl.program_id(1) == 0
    c = pl.program_id(2)
    L = SSD_CHUNK

    @pl.when(c == 0)
    def _():
        if has_h0:
            state[...] = h0_ref[...]
        else:
            state[...] = jnp.zeros_like(state)

    z = dt_ref[...] + dtb_ref[...]
    dt = jnp.maximum(z, 0.0) + jnp.log1p(jnp.exp(-jnp.abs(z)))
    dta = dt * (-jnp.exp(alog_ref[...]))
    qi = lax.broadcasted_iota(I32, (L, L), 0)
    ki = lax.broadcasted_iota(I32, (L, L), 1)
    ahead = jnp.where(fwd, qi - ki, ki - qi)
    causal = ahead >= 0
    causal_t = ahead <= 0
    tri = causal.astype(F32).astype(BF16)
    tri_t = causal_t.astype(F32).astype(BF16)
    a_cum = sum(jnp.dot(tri, p, preferred_element_type=F32) for p in _split3(dta))
    a_cum_t = sum(jnp.dot(p, tri_t, preferred_element_type=F32) for p in _split3(dta.T))
    dt_t = dt.T
    total = jnp.sum(dta, axis=0, keepdims=True)
    lane_lo = lax.broadcasted_iota(I32, (L, HEAD_WIDTH), 1) < SSD_HEAD_DIM
    row_lo = lax.broadcasted_iota(I32, (HEAD_WIDTH, SSD_STATE), 0) < SSD_HEAD_DIM

    n_groups = n_heads // heads_per_group
    for g in range(n_groups):
        gs = slice(g * SSD_STATE, (g + 1) * SSD_STATE)
        b_g = bm_ref[:, gs].astype(BF16)
        c_g = cm_ref[:, gs].astype(BF16)
        cb = _nt_dot(c_g, b_g)
        for j in range(heads_per_group // 2):
            pair = (g * heads_per_group) // 2 + j
            ps = slice(pair * HEAD_WIDTH, (pair + 1) * HEAD_WIDTH)
            x_pair = xs_ref[:, ps]
            ws, acols, tots = [], [], []
            for hh in (2 * pair, 2 * pair + 1):
                acol = a_cum[:, hh:hh + 1]
                seg = acol - a_cum_t[hh:hh + 1, :]
                decay = jnp.exp(jnp.where(causal, seg, NEG_INF))
                ws.append(cb * decay * dt_t[hh:hh + 1, :])
                acols.append(acol)
                tots.append(total[:, hh:hh + 1])
            w2 = jnp.concatenate(ws, axis=1).astype(BF16)
            x2 = jnp.concatenate([jnp.where(lane_lo, x_pair, 0.0), jnp.where(lane_lo, 0.0, x_pair)],
                                 axis=0).astype(BF16)
            y_diag = jnp.dot(w2, x2, preferred_element_type=F32)
            h_in = state[ps, :]
            y_off = _nt_dot(c_g, h_in.astype(BF16)) * jnp.where(lane_lo, jnp.exp(acols[0]), jnp.exp(acols[1]))
            y_ref[:, ps] = y_diag + y_off
            to_end = [jnp.exp(tots[i] - acols[i]) * dt[:, 2 * pair + i: 2 * pair + i + 1] for i in range(2)]
            xw = x_pair * jnp.where(lane_lo, to_end[0], to_end[1])
            st = _tn_dot(xw.astype(BF16), b_g)
            dec = jnp.where(row_lo, jnp.exp(tots[0]), jnp.exp(tots[1]))
            state[ps, :] = h_in * dec + st

    if want_state:
        @pl.when(c == pl.num_programs(2) - 1)
        def _():
            hl_ref[...] = state[...]


def _ssd_scan(xbc, dt_raw, dt_bias, a_log, h0, row0_dt, batch, seq, n_heads, want_state):
    L = SSD_CHUNK
    nc = seq // L
    width = n_heads * SSD_HEAD_DIM
    n_groups = SSD_GROUPS
    bc = n_groups * SSD_STATE
    assert width % bc == 0 and row0_dt % L == 0
    r0 = row0_dt // L

    def rb(b, d, c):
        return b * nc + c + d * (nc - 1 - 2 * c)

    in_specs = [
        pl.BlockSpec((L, width), lambda b, d, c: (rb(b, d, c), 0)),
        pl.BlockSpec((L, bc), lambda b, d, c: (rb(b, d, c), width // bc)),
        pl.BlockSpec((L, bc), lambda b, d, c: (rb(b, d, c), width // bc + 1)),
        pl.BlockSpec((L, LANES), lambda b, d, c: (r0 + rb(b, d, c), d)),
        pl.BlockSpec((None, 1, LANES), lambda b, d, c: (d, 0, 0)),
        pl.BlockSpec((None, 1, LANES), lambda b, d, c: (d, 0, 0)),
    ]
    args = [xbc, xbc, xbc, dt_raw, dt_bias, a_log]
    has_h0 = h0 is not None
    if has_h0:
        in_specs.append(pl.BlockSpec((None, None, width, SSD_STATE), lambda b, d, c: (b, d, 0, 0)))
        args.append(h0)
    out_specs = [pl.BlockSpec((None, L, width), lambda b, d, c: (d, rb(b, d, c), 0))]
    out_shape = [jax.ShapeDtypeStruct((2, batch * seq, width), F32)]
    if want_state:
        out_specs.append(pl.BlockSpec((None, None, width, SSD_STATE), lambda b, d, c: (b, d, 0, 0)))
        out_shape.append(jax.ShapeDtypeStruct((batch, 2, width, SSD_STATE), F32))
    return pl.pallas_call(
        functools.partial(_ssd_kernel, n_heads=n_heads, heads_per_group=n_heads // n_groups,
                          has_h0=has_h0, want_state=want_state),
        grid=(batch, 2, nc),
        in_specs=in_specs,
        out_specs=out_specs,
        out_shape=out_shape,
        scratch_shapes=[pltpu.VMEM((width, SSD_STATE), F32)],
        compiler_params=_cparams(("arbitrary", "arbitrary", "arbitrary")),
        name="ssd_scan",
    )(*args)


def _ssd_gate_kernel(y_ref, xs_ref, z_ref, d_ref, nw_ref, o_ref):
    y = y_ref[0] + y_ref[1] + d_ref[...] * xs_ref[...]
    y = y * _silu(z_ref[...])
    o_ref[...] = _rms(y, nw_ref[...]).astype(o_ref.dtype)


def _ssd_gate(y2, xbc, proj, z_col0, row0, d_full, norm_w, width):
    n_rows = y2.shape[1]
    gw = width // SSD_GROUPS
    tm = _pick(n_rows, (512, 256, 128))
    assert z_col0 % gw == 0 and row0 % tm == 0
    zc, r0 = z_col0 // gw, row0 // tm
    return pl.pallas_call(
        _ssd_gate_kernel,
        grid=(n_rows // tm, SSD_GROUPS),
        in_specs=[
            pl.BlockSpec((2, tm, gw), lambda i, g: (0, i, g)),
            pl.BlockSpec((tm, gw), lambda i, g: (i, g)),
            pl.BlockSpec((tm, gw), lambda i, g: (r0 + i, zc + g)),
            pl.BlockSpec((1, gw), lambda i, g: (0, g)),
            pl.BlockSpec((1, gw), lambda i, g: (0, g)),
        ],
        out_specs=pl.BlockSpec((tm, gw), lambda i, g: (i, g)),
        out_shape=jax.ShapeDtypeStruct((n_rows, width), BF16),
        compiler_params=_cparams(("arbitrary", "arbitrary")),
        name="ssd_gate",
    )(y2, xbc, proj, d_full, norm_w)


def _outproj_kernel(d_ref, g_ref, s_ref, wd_ref, wg_ref, ws_ref, o_ref):
    acc = jnp.dot(d_ref[...], wd_ref[...], preferred_element_type=F32)
    acc += jnp.dot(g_ref[...], wg_ref[...], preferred_element_type=F32)
    acc += jnp.dot(s_ref[...], ws_ref[...], preferred_element_type=F32)
    o_ref[...] = acc


def _outproj(d_out, g_out, s_out, w_out):
    t = d_out.shape[0]
    dw, gw, sw = d_out.shape[1], g_out.shape[1], s_out.shape[1]
    d = w_out.shape[1]
    assert dw == gw and sw % dw == 0 and (dw + gw) % sw == 0
    tm = _pick(t, (512, 256, 128))
    tn = _pick(d, (1024, 512, 256, 128))
    return pl.pallas_call(
        _outproj_kernel,
        grid=(t // tm, d // tn),
        in_specs=[
            pl.BlockSpec((tm, dw), lambda i, j: (i, 0)),
            pl.BlockSpec((tm, gw), lambda i, j: (i, 0)),
            pl.BlockSpec((tm, sw), lambda i, j: (i, 0)),
            pl.BlockSpec((dw, tn), lambda i, j: (0, j)),
            pl.BlockSpec((gw, tn), lambda i, j: (1, j)),
            pl.BlockSpec((sw, tn), lambda i, j: ((dw + gw) // sw, j)),
        ],
        out_specs=pl.BlockSpec((tm, tn), lambda i, j: (i, j)),
        out_shape=jax.ShapeDtypeStruct((t, d), F32),
        compiler_params=_cparams(("arbitrary", "arbitrary")),
        name="out_proj",
    )(d_out, g_out, s_out, w_out, w_out, w_out)


def _first_index(hit, idx, sentinel):
    return jnp.min(jnp.where(hit, idx, sentinel), axis=0, keepdims=True)


def _router_kernel(u_ref, x_ref, mod_ref, g_ref, rw_ref, rb_ref,
                   x1_ref, h2_ref, idx_ref, wts_ref, rank_ref, cnt_ref, carry):
    per_group = N_EXPERTS // N_EXPERT_GROUPS

    @pl.when(pl.program_id(0) == 0)
    def _():
        carry[...] = jnp.zeros_like(carry)

    x1 = x_ref[...] + mod_ref[2:3, :] * _rms(u_ref[...], g_ref[1:2, :])
    x1_ref[...] = x1
    h2 = _rms(x1, g_ref[2:3, :]) * (1.0 + mod_ref[4:5, :]) + mod_ref[3:4, :]
    _slab_store(h2_ref, h2, h2.shape[1] // LANES)
    logits =_nt_dot(rw_ref[...], h2.astype(BF16))
    scores = 1.0 / (1.0 + jnp.exp(-logits))
    sel = scores + rb_ref[...]
    tm = sel.shape[1]
    sub = lax.broadcasted_iota(I32, (per_group, tm), 0)

    sel_g = [sel[g * per_group:(g + 1) * per_group, :] for g in range(N_EXPERT_GROUPS)]
    sc_g = [scores[g * per_group:(g + 1) * per_group, :] for g in range(N_EXPERT_GROUPS)]
    gscore = jnp.zeros((N_EXPERT_GROUPS, tm), F32)
    gsub = lax.broadcasted_iota(I32, (N_EXPERT_GROUPS, tm), 0)
    for g in range(N_EXPERT_GROUPS):
        v = sel_g[g]
        m1 = jnp.max(v, axis=0, keepdims=True)
        i1 = _first_index(v == m1, sub, per_group)
        m2 = jnp.max(jnp.where(sub == i1, NEG_INF, v), axis=0, keepdims=True)
        gscore = jnp.where(gsub == g, m1 + m2, gscore)
    gmask = jnp.zeros((N_EXPERT_GROUPS, tm), F32)
    gv = gscore
    for _ in range(TOPK_GROUPS):
        m = jnp.max(gv, axis=0, keepdims=True)
        hit = gsub == _first_index(gv == m, gsub, N_EXPERT_GROUPS)
        gmask = jnp.where(hit, 1.0, gmask)
        gv = jnp.where(hit, NEG_INF, gv)
    vals = [jnp.where(gmask[g:g + 1, :] > 0.0, sel_g[g], NEG_INF) for g in range(N_EXPERT_GROUPS)]
    eid = [sub + g * per_group for g in range(N_EXPERT_GROUPS)]

    picked = [jnp.zeros((per_group, tm), F32) for _ in range(N_EXPERT_GROUPS)]
    w_rows, idx_rows = [], []
    for _ in range(TOP_K):
        m = functools.reduce(jnp.maximum, [jnp.max(v, axis=0, keepdims=True) for v in vals])
        first = functools.reduce(jnp.minimum,
                                 [_first_index(v == m, e, N_EXPERTS) for v, e in zip(vals, eid)])
        w = jnp.zeros((1, tm), F32)
        for g in range(N_EXPERT_GROUPS):
            hit = eid[g] == first
            w = w + jnp.sum(jnp.where(hit, sc_g[g], 0.0), axis=0, keepdims=True)
            vals[g] = jnp.where(hit, NEG_INF, vals[g])
            picked[g] = jnp.where(hit, 1.0, picked[g])
        w_rows.append(w)
        idx_rows.append(first)
    wsum = functools.reduce(lambda a, b: a + b, w_rows)

    onehot = jnp.concatenate(picked, axis=0)
    ti = lax.broadcasted_iota(I32, (tm, tm), 0)
    tj = lax.broadcasted_iota(I32, (tm, tm), 1)
    upper = (ti < tj).astype(F32).astype(BF16)
    rank = jnp.dot(onehot.astype(BF16), upper, preferred_element_type=F32) + carry[:, 0:1]
    new_carry = carry[...] + jnp.sum(onehot, axis=1, keepdims=True)
    carry[...] = new_carry
    cnt_ref[...] = new_carry
    for k in range(TOP_K):
        idx_ref[k:k + 1, :] = idx_rows[k]
        wts_ref[k:k + 1, :] = w_rows[k] / wsum * ROUTED_SCALE
        r = jnp.zeros((1, tm), F32)
        for g in range(N_EXPERT_GROUPS):
            r = r + jnp.sum(jnp.where(eid[g] == idx_rows[k], rank[g * per_group:(g + 1) * per_group, :], 0.0),
                            axis=0, keepdims=True)
        rank_ref[k:k + 1, :] = r.astype(I32)


def _router(u, x, mod_l, g4, rw_t, rbias, toks):
    t, d = x.shape
    tm = min(256, _pick(toks.n_ctx, (256, 128)), _pick(toks.n_lat_seq, (256, 128)))
    slab = d // LANES
    row = lambda i: (i, 0)
    col = lambda i: (0, i)
    fixed = lambda i: (0, 0)
    return pl.pallas_call(
        _router_kernel,
        grid=(t // tm,),
        in_specs=[
            pl.BlockSpec((tm, d), row),
            pl.BlockSpec((tm, d), row),
            pl.BlockSpec((None, N_MOD, d), lambda i: (toks.mod_row(i, tm), 0, 0)),
            pl.BlockSpec((4, d), fixed),
            pl.BlockSpec((N_EXPERTS, d), fixed),
            pl.BlockSpec((N_EXPERTS, 1), fixed),
        ],
        out_specs=[
            pl.BlockSpec((tm, d), row),
            pl.BlockSpec((tm * slab, LANES), row),
            pl.BlockSpec((TOP_K, tm), col),
            pl.BlockSpec((TOP_K, tm), col),
            pl.BlockSpec((TOP_K, tm), col),
            pl.BlockSpec((N_EXPERTS, LANES), fixed),
        ],
        out_shape=[
            jax.ShapeDtypeStruct((t, d), F32),
            jax.ShapeDtypeStruct((t * slab, LANES), F32),
            jax.ShapeDtypeStruct((TOP_K, t), I32),
            jax.ShapeDtypeStruct((TOP_K, t), F32),
            jax.ShapeDtypeStruct((TOP_K, t), I32),
            jax.ShapeDtypeStruct((N_EXPERTS, LANES), F32),
        ],
        scratch_shapes=[pltpu.VMEM((N_EXPERTS, LANES), F32)],
        compiler_params=_cparams(("arbitrary",)),
        name="router",
    )(u, x, mod_l, g4, rw_t, rbias)


def _dispatch_kernel(fill_ref, dest_ref, h_ref, hs_ref, zbuf, sem, *, slab):
    tchunk = dest_ref.shape[1]
    base = pl.program_id(0) * tchunk
    blk = zbuf.shape[0]

    def row_copy(t, d):
        src = h_ref.at[pl.ds(pl.multiple_of(t * slab, slab), slab)]
        dst = hs_ref.at[pl.ds(pl.multiple_of(d * slab, slab), slab)]
        return pltpu.make_async_copy(src, dst, sem.at[0])

    @pl.when(pl.program_id(0) == 0)
    def _():
        zbuf[...] = jnp.zeros_like(zbuf)

        def fill_copy(e):
            start = pl.multiple_of(fill_ref[e] * slab, blk)
            return pltpu.make_async_copy(zbuf, hs_ref.at[pl.ds(start, blk)], sem.at[1])

        def start(e, carry):
            @pl.when(fill_ref[e] >= 0)
            def _():
                fill_copy(e).start()
            return carry

        def wait(e, carry):
            @pl.when(fill_ref[e] >= 0)
            def _():
                fill_copy(e).wait()
            return carry

        lax.fori_loop(0, N_EXPERTS, start, 0)
        lax.fori_loop(0, N_EXPERTS, wait, 0)

    def issue(t, carry):
        for k in range(TOP_K):
            row_copy(base + t, dest_ref[k, t]).start()
        return carry

    def drain(t, carry):
        for k in range(TOP_K):
            row_copy(0, 0).wait()
        return carry

    lax.fori_loop(0, tchunk, issue, 0)
    lax.fori_loop(0, tchunk, drain, 0)


def _dispatch(h2, dest, fill_start, m_rows, slab):
    t = h2.shape[0] // slab
    tchunk = _pick(t, (512, 256, 128))
    return pl.pallas_call(
        functools.partial(_dispatch_kernel, slab=slab),
        grid_spec=pltpu.PrefetchScalarGridSpec(
            num_scalar_prefetch=1,
            grid=(t // tchunk,),
            in_specs=[
                pl.BlockSpec((TOP_K, tchunk), lambda i, fill: (0, i), memory_space=pltpu.SMEM),
                pl.BlockSpec(memory_space=pl.ANY),
            ],
            out_specs=pl.BlockSpec(memory_space=pl.ANY),
            scratch_shapes=[pltpu.VMEM((EXPERT_ROWS * slab, LANES), F32), pltpu.SemaphoreType.DMA((2,))],
        ),
        out_shape=jax.ShapeDtypeStruct((m_rows * slab, LANES), F32),
        compiler_params=_cparams(("arbitrary",)),
        name="moe_dispatch",
    )(fill_start, dest, h2)


def _expert_kernel(blk_e_ref, n_used_ref, hs_ref, wg_ref, wu_ref, wd_ref, y_ref, *, slab):
    @pl.when(pl.program_id(0) < n_used_ref[0])
    def _():
        h = _slab_load(hs_ref, EXPERT_ROWS, slab).astype(BF16)
        a = jnp.dot(h, wg_ref[...], preferred_element_type=F32)
        b = jnp.dot(h, wu_ref[...], preferred_element_type=F32)
        y = jnp.dot((_silu(a) * b).astype(BF16), wd_ref[...], preferred_element_type=F32)
        _slab_store(y_ref, y, slab)


def _experts(hs, blk_e, n_used, wg, wu, wd):
    d, ff = wg.shape[1], wg.shape[2]
    slab = d // LANES
    m_rows = hs.shape[0] // slab
    n_blk = m_rows // EXPERT_ROWS

    def blk(i, blk_e, n_used):
        return jnp.minimum(i, n_used[0] - 1)

    return pl.pallas_call(
        functools.partial(_expert_kernel, slab=slab),
        grid_spec=pltpu.PrefetchScalarGridSpec(
            num_scalar_prefetch=2,
            grid=(n_blk,),
            in_specs=[
                pl.BlockSpec((EXPERT_ROWS * slab, LANES), lambda i, be, nu: (blk(i, be, nu), 0)),
                pl.BlockSpec((None, d, ff), lambda i, be, nu: (be[blk(i, be, nu)], 0, 0)),
                pl.BlockSpec((None, d, ff), lambda i, be, nu: (be[blk(i, be, nu)], 0, 0)),
                pl.BlockSpec((None, ff, d), lambda i, be, nu: (be[blk(i, be, nu)], 0, 0)),
            ],
            out_specs=pl.BlockSpec((EXPERT_ROWS * slab, LANES), lambda i, be, nu: (blk(i, be, nu), 0)),
        ),
        out_shape=jax.ShapeDtypeStruct((m_rows * slab, LANES), F32),
        compiler_params=_cparams(("arbitrary",)),
        name="moe_experts",
    )(blk_e, n_used, hs, wg, wu, wd)


def _shared_kernel(h_ref, wg_ref, wu_ref, wd_ref, o_ref):
    slab = wg_ref.shape[0] // LANES
    h = _slab_load(h_ref, o_ref.shape[0], slab).astype(BF16)
    a = jnp.dot(h, wg_ref[...], preferred_element_type=F32)
    b = jnp.dot(h, wu_ref[...], preferred_element_type=F32)
    o_ref[...] = jnp.dot((_silu(a) * b).astype(BF16), wd_ref[...], preferred_element_type=F32)


def _shared_expert(h2, wg, wu, wd):
    d, ff = wg.shape
    slab = d // LANES
    t = h2.shape[0] // slab
    tm = _pick(t, (256, 128))
    fixed = lambda i: (0, 0)
    return pl.pallas_call(
        _shared_kernel,
        grid=(t // tm,),
        in_specs=[
            pl.BlockSpec((tm * slab, LANES), lambda i: (i, 0)),
            pl.BlockSpec((d, ff), fixed),
            pl.BlockSpec((d, ff), fixed),
            pl.BlockSpec((ff, d), fixed),
        ],
        out_specs=pl.BlockSpec((tm, d), lambda i: (i, 0)),
        out_shape=jax.ShapeDtypeStruct((t, d), F32),
        compiler_params=_cparams(("arbitrary",)),
        name="shared_expert",
    )(h2, wg, wu, wd)


def _combine_kernel(dest_ref, y_ref, w_ref, sh_ref, x1_ref, mod_ref, g_ref, o_ref, ybuf, sem):
    tm, d_model = x1_ref.shape
    slab = d_model // LANES

    def row_copy(k, t, d):
        src = y_ref.at[pl.ds(pl.multiple_of(d * slab, slab), slab)]
        dst = ybuf.at[k, pl.ds(pl.multiple_of(t * slab, slab), slab)]
        return pltpu.make_async_copy(src, dst, sem.at[0])

    def issue(t, carry):
        for k in range(TOP_K):
            row_copy(k, t, dest_ref[k, t]).start()
        return carry

    def drain(t, carry):
        for k in range(TOP_K):
            row_copy(0, 0, 0).wait()
        return carry

    lax.fori_loop(0, tm, issue, 0)
    lax.fori_loop(0, tm, drain, 0)
    f = sh_ref[...]
    for k in range(TOP_K):
        f = f + _slab_load(ybuf, tm, slab, lead=(k,)) * w_ref[:, k:k + 1]
    o_ref[...] = x1_ref[...] + mod_ref[5:6, :] * _rms(f, g_ref[3:4, :])


def _combine(y, dest, wts_t, sh, x1, mod_l, g4, toks):
    t, d = x1.shape
    tm = 128
    return pl.pallas_call(
        _combine_kernel,
        grid=(t // tm,),
        in_specs=[
            pl.BlockSpec((TOP_K, tm), lambda i: (0, i), memory_space=pltpu.SMEM),
            pl.BlockSpec(memory_space=pl.ANY),
            pl.BlockSpec((tm, TOP_K), lambda i: (i, 0)),
            pl.BlockSpec((tm, d), lambda i: (i, 0)),
            pl.BlockSpec((tm, d), lambda i: (i, 0)),
            pl.BlockSpec((None, N_MOD, d), lambda i: (toks.mod_row(i, tm), 0, 0)),
            pl.BlockSpec((4, d), lambda i: (0, 0)),
        ],
        out_specs=pl.BlockSpec((tm, d), lambda i: (i, 0)),
        out_shape=jax.ShapeDtypeStruct((t, d), F32),
        scratch_shapes=[pltpu.VMEM((TOP_K, tm * (d // LANES), LANES), F32), pltpu.SemaphoreType.DMA((1,))],
        compiler_params=_cparams(("arbitrary",)),
        name="moe_combine",
    )(dest, y, wts_t, sh, x1, mod_l, g4)


def _axial_tables(n_tokens, dim):
    rows = n_tokens // GRID_W
    row = jnp.repeat(jnp.arange(rows), GRID_W).astype(F32)
    col = (jnp.arange(rows * GRID_W) % GRID_W).astype(F32)
    n_freq = dim // 4
    inv = jnp.exp(-math.log(ROPE_THETA) * jnp.arange(n_freq, dtype=F32) / n_freq)
    ang = jnp.concatenate([row[:, None] * inv, col[:, None] * inv], axis=-1)
    cos, sin = jnp.cos(ang), jnp.sin(ang)
    reps = HEAD_WIDTH // dim
    cos_full = jnp.tile(jnp.concatenate([cos, cos], axis=-1), (1, reps))
    sin_full = jnp.tile(jnp.concatenate([-sin, sin], axis=-1), (1, reps))
    return cos_full, sin_full


def kernel(x_prompt, x_sample, cache_diff_k, cache_diff_v, cache_gqa_k, cache_gqa_v, state_ssd, c, c_ctx, w_ada, b_ada, norm_g, w_in, w_out, diff_lambda, diff_subnorm, gqa_qk_norm, ssd_conv_w, ssd_conv_b, ssd_dt_bias, ssd_a_log, ssd_d, ssd_norm, router_w, router_bias, exp_w_gate, exp_w_up, exp_w_down, sh_w_gate, sh_w_up, sh_w_down):
    batch, seq, d = x_prompt.shape
    dec_batch, dec_seq, _ = x_sample.shape
    depth = w_ada.shape[0]
    past = cache_diff_k.shape[2]
    n_dh = cache_diff_k.shape[3]
    n_kv = cache_gqa_k.shape[3]
    n_gh = (w_out.shape[1] - n_dh * HEAD_WIDTH - ssd_norm.shape[1]) // HEAD_WIDTH
    ssd_w = ssd_norm.shape[1]
    n_sh = ssd_w // SSD_HEAD_DIM
    bc = SSD_GROUPS * SSD_STATE
    dw, gw, kw = n_dh * HEAD_WIDTH, n_gh * HEAD_WIDTH, n_kv * HEAD_WIDTH
    assert 1 + dec_batch <= 8 and n_sh <= LANES

    t_ctx, t_lat = batch * seq, dec_batch * dec_seq
    t_all = t_ctx + t_lat
    toks = _Tokens(t_ctx, dec_seq)
    x = jnp.concatenate([x_prompt.reshape(t_ctx, d), x_sample.reshape(t_lat, d)], axis=0)

    cond = jnp.zeros((8, d), F32).at[0].set(c_ctx).at[1:1 + dec_batch].set(c)
    mod = _ada(cond, w_ada, b_ada).reshape(depth, 8, N_MOD, d)

    z_col = 3 * dw + gw + 2 * kw
    xbc_col = z_col + ssd_w
    dt_col = xbc_col + ssd_w + 2 * bc
    rope_d = _axial_tables(dec_seq, HEAD_WIDTH // 2)
    rope_g = _axial_tables(dec_seq, HEAD_WIDTH)

    tk_all = t_all * TOP_K
    n_blk = tk_all // EXPERT_ROWS + N_EXPERTS
    m_rows = n_blk * EXPERT_ROWS

    new_dk, new_dv, new_gk, new_gv, new_st = [], [], [], [], []
    for li in range(depth):
        w_in_l = w_in[li]
        w_main = w_in_l[:, :dt_col].astype(BF16)
        w_dt = jnp.zeros((d, 2 * LANES), F32)
        w_dt = w_dt.at[:, :n_sh].set(w_in_l[:, dt_col:dt_col + n_sh])
        w_dt = w_dt.at[:, LANES:LANES + n_sh].set(w_in_l[:, dt_col + n_sh:]).astype(BF16)
        proj, dt_raw = _inproj(x, mod[li], norm_g[li, 0:1], w_main, w_dt, toks)

        pad_heads = lambda v: jnp.zeros((2, 1, LANES), F32).at[:, 0, :n_sh].set(v)
        dt_bias, a_log = pad_heads(ssd_dt_bias[li]), pad_heads(ssd_a_log[li])
        d_full = jnp.repeat(ssd_d[li], SSD_HEAD_DIM)[None, :]
        norm_w = ssd_norm[li][None, :]
        lam_init = 0.8 - 0.6 * math.exp(-0.3 * li)
        diff_args = (diff_lambda[li], diff_subnorm[li][None, :], lam_init)

        qd, kd, vd, qg, kg, vg, kg_f32 = _prep(proj, gqa_qk_norm[li], 0, t_ctx, seq, n_dh, n_gh, n_kv, None)
        as_seq = lambda a, b, n: a.reshape(b, n, a.shape[-1])
        d_ctx = _attention(qd, [(as_seq(kd, batch, seq), as_seq(vd, batch, seq), seq)], n_dh, 1, batch, seq,
                           diff=diff_args)
        g_ctx = _attention(qg, [(as_seq(kg, batch, seq), as_seq(vg, batch, seq), seq)], n_gh, n_gh // n_kv,
                           batch, seq)
        xbc_c = _conv_silu(proj, xbc_col, ssd_w + 2 * bc, ssd_conv_w[li], ssd_conv_b[li][None, :], 0, batch, seq)
        y_c, h_last = _ssd_scan(xbc_c, dt_raw, dt_bias, a_log, None, 0, batch, seq, n_sh, True)
        s_ctx = _ssd_gate(y_c, xbc_c, proj, z_col, 0, d_full, norm_w, ssd_w)
        new_dk.append(proj[:t_ctx, dw:2 * dw].reshape(batch, seq, n_dh, 2, HEAD_WIDTH // 2))
        new_dv.append(proj[:t_ctx, 2 * dw:3 * dw].reshape(batch, seq, n_dh, HEAD_WIDTH))
        new_gk.append(kg_f32.reshape(batch, seq, n_kv, HEAD_WIDTH))
        new_gv.append(proj[:t_ctx, 3 * dw + gw + kw: 3 * dw + gw + 2 * kw].reshape(batch, seq, n_kv, HEAD_WIDTH))
        new_st.append(h_last.reshape(batch, 2, n_sh, SSD_HEAD_DIM, SSD_STATE))

        qd, kd, vd, qg, kg, vg = _prep(proj, gqa_qk_norm[li], t_ctx, t_lat, dec_seq, n_dh, n_gh, n_kv,
                                       rope_d + rope_g)
        c_dk = cache_diff_k[:, li].reshape(dec_batch, past, dw)
        c_dv = cache_diff_v[:, li].reshape(dec_batch, past, dw)
        c_gk = cache_gqa_k[:, li].reshape(dec_batch, past, kw)
        c_gv = cache_gqa_v[:, li].reshape(dec_batch, past, kw)
        d_lat = _attention(qd, [(c_dk, c_dv, past), (as_seq(kd, dec_batch, dec_seq), as_seq(vd, dec_batch, dec_seq), dec_seq)],
                           n_dh, 1, dec_batch, dec_seq, diff=diff_args)
        g_lat = _attention(qg, [(c_gk, c_gv, past), (as_seq(kg, dec_batch, dec_seq), as_seq(vg, dec_batch, dec_seq), dec_seq)],
                           n_gh, n_gh // n_kv, dec_batch, dec_seq)
        xbc_l = _conv_silu(proj, xbc_col, ssd_w + 2 * bc, ssd_conv_w[li], ssd_conv_b[li][None, :], t_ctx,
                           dec_batch, dec_seq)
        h0 = state_ssd[:, li].reshape(dec_batch, 2, ssd_w, SSD_STATE)
        (y_l,) = _ssd_scan(xbc_l, dt_raw, dt_bias, a_log, h0, t_ctx, dec_batch, dec_seq, n_sh, False)
        s_lat = _ssd_gate(y_l, xbc_l, proj, z_col, t_ctx, d_full, norm_w, ssd_w)

        cat = lambda a, b: jnp.concatenate([a, b], axis=0)
        u = _outproj(cat(d_ctx, d_lat), cat(g_ctx, g_lat), cat(s_ctx, s_lat), w_out[li].astype(BF16))
        x1, h2, idx, wts, rank, counts = _router(u, x, mod[li], norm_g[li], router_w[li].T.astype(BF16),
                                                 router_bias[li][:, None], toks)

        counts = counts[:, 0].astype(I32)
        padded = (counts + EXPERT_ROWS - 1) // EXPERT_ROWS * EXPERT_ROWS
        pad_end = jnp.cumsum(padded)
        pad_start = pad_end - padded
        dest = jnp.take(pad_start, idx, axis=0) + rank
        fill_start = jnp.where(counts > 0, pad_end - EXPERT_ROWS, -1).astype(I32)
        n_used = (pad_end[-1:] // EXPERT_ROWS).astype(I32)
        blk_e = jnp.minimum(jnp.searchsorted(pad_end, jnp.arange(n_blk, dtype=I32) * EXPERT_ROWS, side="right"),
                            N_EXPERTS - 1).astype(I32)

        hs = _dispatch(h2, dest, fill_start, m_rows, d // LANES)
        y = _experts(hs, blk_e, n_used, exp_w_gate[li].astype(BF16), exp_w_up[li].astype(BF16),
                     exp_w_down[li].astype(BF16))
        sh = _shared_expert(h2, sh_w_gate[li].astype(BF16), sh_w_up[li].astype(BF16), sh_w_down[li].astype(BF16))
        x = _combine(y, dest, wts.T, sh, x1, mod[li], norm_g[li], toks)

    y_prompt = x[:t_ctx].reshape(batch, seq, d)
    y_sample = x[t_ctx:].reshape(dec_batch, dec_seq, d)
    stack = lambda parts: jnp.stack(parts, axis=1)
    return (y_prompt, y_sample, stack(new_dk), stack(new_dv), stack(new_gk), stack(new_gv), stack(new_st))
```

```python
import functools
import math

import jax
import jax.numpy as jnp
from jax import lax
from jax.experimental import pallas as pl
from jax.experimental.pallas import tpu as pltpu

F32 = jnp.float32
BF16 = jnp.bfloat16
I32 = jnp.int32

LANES = 128
HEAD_WIDTH = 128
GRID_W = 64
GQA_KV_HEADS = 2
SSD_HEAD_DIM = 64
SSD_GROUPS = 4
SSD_STATE = 128
SSD_CHUNK = 128
ROPE_THETA = 10000.0
RMS_EPS = 1e-6
N_MOD = 6
N_EXPERTS = 64
TOP_K = 8
N_EXPERT_GROUPS = 8
TOPK_GROUPS = 4
ROUTED_SCALE = 2.5
EXPERT_ROWS = 256
VMEM_LIMIT = 56 * 1024 * 1024
NEG_INF = float("-inf")


def _cparams(sem):
    return pltpu.CompilerParams(dimension_semantics=sem, vmem_limit_bytes=VMEM_LIMIT)


def _pick(n, cands):
    for c in cands:
        if n % c == 0:
            return c
    raise ValueError(f"no tile for {n} in {cands}")


def _silu(x):
    return x * (1.0 / (1.0 + jnp.exp(-x)))


def _rms(x, g):
    return x * lax.rsqrt(jnp.mean(x * x, axis=-1, keepdims=True) + RMS_EPS) * g


def _nt_dot(a, b):
    return lax.dot_general(a, b, (((1,), (1,)), ((), ())), preferred_element_type=F32)


def _tn_dot(a, b):
    return lax.dot_general(a, b, (((0,), (0,)), ((), ())), preferred_element_type=F32)


HI_MASK = -65536


def _slab_pitch(d):
    slab = d // (2 * LANES)
    assert slab % 8 == 0
    return slab, (slab if (slab // 8) % 2 == 1 else slab + 8)


def _pack_rows(v):
    half = v.shape[1] // 2
    lo = lax.bitcast_convert_type(v[:, :half].astype(BF16).astype(F32), I32)
    hi = lax.bitcast_convert_type(v[:, half:].astype(BF16).astype(F32), I32)
    return lax.shift_right_logical(lo, 16) | (hi & HI_MASK)


def _unpack_rows(w):
    lo = lax.bitcast_convert_type(lax.shift_left(w, 16), F32)
    hi = lax.bitcast_convert_type(w & HI_MASK, F32)
    return lo, hi


def _slab_load(ref, n, slab, pitch, lead=()):
    return jnp.concatenate([ref[lead + (pl.ds(a, n, stride=pitch), slice(None))] for a in range(slab)], axis=1)


def _slab_store(ref, words, slab, pitch):
    n = words.shape[0]
    for a in range(slab):
        ref[pl.ds(a, n, stride=pitch), :] = words[:, a * LANES:(a + 1) * LANES]
    for a in range(slab, pitch):
        ref[pl.ds(a, n, stride=pitch), :] = jnp.zeros((n, LANES), I32)


def _ada_kernel(cond_ref, w_ref, b_ref, o_ref):
    s = _silu(cond_ref[...]).astype(BF16)
    o_ref[0] = jnp.dot(s, w_ref[0].astype(BF16), preferred_element_type=F32) + b_ref[0]


def _ada(cond, w_ada, b_ada):
    depth, d, n = w_ada.shape
    tn = _pick(n, (512, 256, 128))
    return pl.pallas_call(
        _ada_kernel,
        grid=(depth, n // tn),
        in_specs=[
            pl.BlockSpec((8, d), lambda l, j: (0, 0)),
            pl.BlockSpec((1, d, tn), lambda l, j: (l, 0, j)),
            pl.BlockSpec((1, 1, tn), lambda l, j: (l, 0, j)),
        ],
        out_specs=pl.BlockSpec((1, 8, tn), lambda l, j: (l, 0, j)),
        out_shape=jax.ShapeDtypeStruct((depth, 8, n), F32),
        compiler_params=_cparams(("arbitrary", "arbitrary")),
        name="ada_mod",
    )(cond, w_ada, b_ada.reshape(depth, 1, n))


class _Tokens:
    def __init__(self, n_ctx, n_lat_seq):
        self.n_ctx = n_ctx
        self.n_lat_seq = n_lat_seq

    def mod_row(self, i, tm):
        assert self.n_ctx % tm == 0 and self.n_lat_seq % tm == 0
        ctx_tiles = self.n_ctx // tm
        per_b = self.n_lat_seq // tm
        return jnp.where(i < ctx_tiles, 0, 1 + (i - ctx_tiles) // per_b)


def _inproj_kernel(x_ref, mod_ref, g_ref, w_ref, wdt_ref, o_ref, odt_ref, h_scr):
    @pl.when(pl.program_id(1) == 0)
    def _():
        h = _rms(x_ref[...], g_ref[...]) * (1.0 + mod_ref[1:2, :]) + mod_ref[0:1, :]
        hb = h.astype(BF16)
        h_scr[...] = hb
        odt_ref[...] = jnp.dot(hb, wdt_ref[...], preferred_element_type=F32)

    o_ref[...] = jnp.dot(h_scr[...], w_ref[...], preferred_element_type=F32)


def _inproj(x, mod_l, g0, w_main, w_dt, toks):
    t, d = x.shape
    n = w_main.shape[1]
    tm = _pick(t, (512, 256, 128))
    tm = min(tm, _pick(toks.n_ctx, (512, 256, 128)), _pick(toks.n_lat_seq, (512, 256, 128)))
    tn = _pick(n, (512, 256, 128))
    ndt = w_dt.shape[1]
    return pl.pallas_call(
        _inproj_kernel,
        grid=(t // tm, n // tn),
        in_specs=[
            pl.BlockSpec((tm, d), lambda i, j: (i, 0)),
            pl.BlockSpec((None, N_MOD, d), lambda i, j: (toks.mod_row(i, tm), 0, 0)),
            pl.BlockSpec((1, d), lambda i, j: (0, 0)),
            pl.BlockSpec((d, tn), lambda i, j: (0, j)),
            pl.BlockSpec((d, ndt), lambda i, j: (0, 0)),
        ],
        out_specs=[
            pl.BlockSpec((tm, tn), lambda i, j: (i, j)),
            pl.BlockSpec((tm, ndt), lambda i, j: (i, 0)),
        ],
        out_shape=[jax.ShapeDtypeStruct((t, n), F32), jax.ShapeDtypeStruct((t, ndt), F32)],
        scratch_shapes=[pltpu.VMEM((tm, d), BF16)],
        compiler_params=_cparams(("arbitrary", "arbitrary")),
        name="in_proj",
    )(x, mod_l, g0, w_main, w_dt)


def _rope128(x, cos, sin, half):
    if half == 64:
        partner = pltpu.roll(x, 64, 1)
    else:
        lane = lax.broadcasted_iota(I32, x.shape, 1)
        partner = jnp.where((lane & (2 * half - 1)) < half, pltpu.roll(x, LANES - half, 1), pltpu.roll(x, half, 1))
    return x * cos + partner * sin


def _prep_kernel(*refs, n_dh, n_gh, n_kv, rope):
    if rope:
        p_ref, qkn_ref, cd_ref, sd_ref, cg_ref, sg_ref, qd_ref, kd_ref, vd_ref, qg_ref, kg_ref, vg_ref = refs
    else:
        p_ref, qkn_ref, qd_ref, kd_ref, vd_ref, qg_ref, kg_ref, vg_ref, kgf_ref = refs
    dw = n_dh * HEAD_WIDTH
    gw = n_gh * HEAD_WIDTH
    kw = n_kv * HEAD_WIDTH
    d_scale = (HEAD_WIDTH // 2) ** -0.5
    g_scale = HEAD_WIDTH ** -0.5
    for h in range(n_dh):
        sl = slice(h * HEAD_WIDTH, (h + 1) * HEAD_WIDTH)
        q = p_ref[:, sl]
        k = p_ref[:, dw + h * HEAD_WIDTH: dw + (h + 1) * HEAD_WIDTH]
        if rope:
            q = _rope128(q, cd_ref[...], sd_ref[...], HEAD_WIDTH // 4)
            k = _rope128(k, cd_ref[...], sd_ref[...], HEAD_WIDTH // 4)
        qd_ref[:, sl] = (q * d_scale).astype(BF16)
        kd_ref[:, sl] = k.astype(BF16)
    vd_ref[...] = p_ref[:, 2 * dw: 3 * dw].astype(BF16)
    off = 3 * dw
    for h in range(n_gh):
        sl = slice(h * HEAD_WIDTH, (h + 1) * HEAD_WIDTH)
        q = _rms(p_ref[:, off + h * HEAD_WIDTH: off + (h + 1) * HEAD_WIDTH], qkn_ref[0:1, :])
        if rope:
            q = _rope128(q, cg_ref[...], sg_ref[...], HEAD_WIDTH // 2)
        qg_ref[:, sl] = (q * g_scale).astype(BF16)
    off += gw
    for h in range(n_kv):
        sl = slice(h * HEAD_WIDTH, (h + 1) * HEAD_WIDTH)
        k = _rms(p_ref[:, off + h * HEAD_WIDTH: off + (h + 1) * HEAD_WIDTH], qkn_ref[1:2, :])
        if rope:
            k = _rope128(k, cg_ref[...], sg_ref[...], HEAD_WIDTH // 2)
        else:
            kgf_ref[:, sl] = k
        kg_ref[:, sl] = k.astype(BF16)
    off += kw
    vg_ref[...] = p_ref[:, off: off + kw].astype(BF16)


def _prep(proj, qk_norm, row0, n_rows, seq, n_dh, n_gh, n_kv, rope_tabs):
    dw, gw, kw = n_dh * HEAD_WIDTH, n_gh * HEAD_WIDTH, n_kv * HEAD_WIDTH
    width = 3 * dw + gw + 2 * kw
    tn = _pick(seq, (256, 128))
    assert row0 % tn == 0
    r0 = row0 // tn
    rope = rope_tabs is not None
    in_specs = [
        pl.BlockSpec((tn, width), lambda i: (r0 + i, 0)),
        pl.BlockSpec((2, HEAD_WIDTH), lambda i: (0, 0)),
    ]
    args = [proj, qk_norm]
    if rope:
        per_seq = seq // tn
        for tab in rope_tabs:
            in_specs.append(pl.BlockSpec((tn, HEAD_WIDTH), lambda i: (i % per_seq, 0)))
            args.append(tab)
    widths = [dw, dw, dw, gw, kw, kw]
    out_specs = [pl.BlockSpec((tn, w), lambda i: (i, 0)) for w in widths]
    out_shape = [jax.ShapeDtypeStruct((n_rows, w), BF16) for w in widths]
    if not rope:
        out_specs.append(pl.BlockSpec((tn, kw), lambda i: (i, 0)))
        out_shape.append(jax.ShapeDtypeStruct((n_rows, kw), F32))
    return pl.pallas_call(
        functools.partial(_prep_kernel, n_dh=n_dh, n_gh=n_gh, n_kv=n_kv, rope=rope),
        grid=(n_rows // tn,),
        in_specs=in_specs,
        out_specs=out_specs,
        out_shape=out_shape,
        compiler_params=_cparams(("arbitrary",)),
        name="attn_prep_rope" if rope else "attn_prep",
    )(*args)


def _softmax_step(state, s, v):
    m, l, acc = state
    m_new = jnp.maximum(m, jnp.max(s, axis=-1, keepdims=True))
    alpha = jnp.exp(m - m_new)
    p = jnp.exp(s - m_new)
    l = alpha * l + jnp.sum(p, axis=-1, keepdims=True)
    acc = alpha * acc + jnp.dot(p.astype(BF16), v, preferred_element_type=F32)
    return m_new, l, acc


def _attn_kernel(*refs, seg_lens, diff, lam_init):
    n_seg = len(seg_lens)
    q_ref = refs[0]
    kv_refs = refs[1: 1 + 2 * n_seg]
    if diff:
        lam_ref, sub_ref, o_ref = refs[1 + 2 * n_seg:]
    else:
        (o_ref,) = refs[1 + 2 * n_seg:]
    q = q_ref[...]
    tq = q.shape[0]
    if diff:
        lane = lax.broadcasted_iota(I32, q.shape, 1)
        qs = [jnp.where(lane < HEAD_WIDTH // 2, q, jnp.zeros_like(q)),
              jnp.where(lane < HEAD_WIDTH // 2, jnp.zeros_like(q), q)]
    else:
        qs = [q]
    init = (jnp.full((tq, 1), NEG_INF, F32), jnp.zeros((tq, 1), F32), jnp.zeros((tq, HEAD_WIDTH), F32))
    states = [init for _ in qs]

    for si, t_len in enumerate(seg_lens):
        k_ref, v_ref = kv_refs[2 * si], kv_refs[2 * si + 1]
        tk = _pick(t_len, (512, 256, 128))

        def chunk(c, sts, k_ref=k_ref, v_ref=v_ref, tk=tk):
            start = pl.multiple_of(c * tk, tk)
            k = k_ref[pl.ds(start, tk), :].astype(BF16)
            v = v_ref[pl.ds(start, tk), :].astype(BF16)
            return tuple(_softmax_step(st, _nt_dot(qm, k), v) for st, qm in zip(sts, qs))

        if t_len == tk:
            states = list(chunk(0, tuple(states)))
        else:
            states = list(lax.fori_loop(0, t_len // tk, chunk, tuple(states)))

    outs = [acc / l for (_, l, acc) in states]
    if diff:
        lp = lam_ref[...]
        lam = (jnp.exp(jnp.sum(lp[0:1] * lp[1:2], axis=-1, keepdims=True))
               - jnp.exp(jnp.sum(lp[2:3] * lp[3:4], axis=-1, keepdims=True)) + lam_init)
        o = outs[0] - lam * outs[1]
        o = _rms(o, sub_ref[...]) * (1.0 - lam_init)
    else:
        o = outs[0]
    o_ref[...] = o.astype(o_ref.dtype)


def _attention(q, kvs, n_heads, kv_group, batch, seq, diff=None):
    tq = _pick(seq, (256, 128))
    per_seq = seq // tq
    in_specs = [pl.BlockSpec((tq, HEAD_WIDTH), lambda b, h, i: (b * per_seq + i, h))]
    args = [q]
    seg_lens = []
    for k, v, t_len in kvs:
        for a in (k, v):
            in_specs.append(pl.BlockSpec((None, t_len, HEAD_WIDTH), lambda b, h, i: (b, 0, h // kv_group)))
            args.append(a)
        seg_lens.append(t_len)
    lam_init = 0.0
    if diff is not None:
        lam_params, subnorm, lam_init = diff
        in_specs.append(pl.BlockSpec(lam_params.shape, lambda b, h, i: (0, 0)))
        in_specs.append(pl.BlockSpec((1, HEAD_WIDTH), lambda b, h, i: (0, 0)))
        args += [lam_params, subnorm]
    return pl.pallas_call(
        functools.partial(_attn_kernel, seg_lens=tuple(seg_lens), diff=diff is not None, lam_init=lam_init),
        grid=(batch, n_heads, per_seq),
        in_specs=in_specs,
        out_specs=pl.BlockSpec((tq, HEAD_WIDTH), lambda b, h, i: (b * per_seq + i, h)),
        out_shape=jax.ShapeDtypeStruct((batch * seq, n_heads * HEAD_WIDTH), BF16),
        compiler_params=_cparams(("arbitrary", "arbitrary", "arbitrary")),
        name="diff_attn" if diff is not None else "gqa_attn",
    )(*args)


def _conv_kernel(u_ref, w_ref, b_ref, o_ref):
    u = u_ref[...]
    n = u.shape[0]
    row = lax.broadcasted_iota(I32, u.shape, 0)
    prev = jnp.where(row == 0, 0.0, pltpu.roll(u, 1, 0))
    nxt = jnp.where(row == n - 1, 0.0, pltpu.roll(u, n - 1, 0))
    y = w_ref[0:1, :] * prev + w_ref[1:2, :] * u + w_ref[2:3, :] * nxt + b_ref[...]
    o_ref[...] = _silu(y)


def _conv_silu(proj, col0, width, conv_w, conv_b, row0, batch, seq):
    tc = 256
    assert col0 % tc == 0 and width % tc == 0 and row0 % seq == 0
    c0, r0 = col0 // tc, row0 // seq
    return pl.pallas_call(
        _conv_kernel,
        grid=(batch, width // tc),
        in_specs=[
            pl.BlockSpec((seq, tc), lambda b, j: (r0 + b, c0 + j)),
            pl.BlockSpec((conv_w.shape[0], tc), lambda b, j: (0, j)),
            pl.BlockSpec((1, tc), lambda b, j: (0, j)),
        ],
        out_specs=pl.BlockSpec((seq, tc), lambda b, j: (b, j)),
        out_shape=jax.ShapeDtypeStruct((batch * seq, width), F32),
        compiler_params=_cparams(("arbitrary", "arbitrary")),
        name="ssd_conv",
    )(proj, conv_w, conv_b)


def _split3(x):
    hi = x.astype(BF16)
    r = x - hi.astype(F32)
    mid = r.astype(BF16)
    lo = (r - mid.astype(F32)).astype(BF16)
    return hi, mid, lo


def _ssd_kernel(*refs, n_heads, heads_per_group, has_h0, want_state):
    it = iter(refs)
    xs_ref, bm_ref, cm_ref, dt_ref, dtb_ref, alog_ref = (next(it) for _ in range(6))
    h0_ref = next(it) if has_h0 else None
    y_ref = next(it)
    hl_ref = next(it) if want_state else None
    state = next(it)

    fwd = pl.program_id(1) == 0
    c = pl.program_id(2)
    L = SSD_CHUNK

    @pl.when(c == 0)
    def _():
        if has_h0:
            state[...] = h0_ref[...]
        else:
            state[...] = jnp.zeros_like(state)

    z = dt_ref[...] + dtb_ref[...]
    dt = jnp.maximum(z, 0.0) + jnp.log1p(jnp.exp(-jnp.abs(z)))
    dta = dt * (-jnp.exp(alog_ref[...]))
    qi = lax.broadcasted_iota(I32, (L, L), 0)
    ki = lax.broadcasted_iota(I32, (L, L), 1)
    ahead = jnp.where(fwd, qi - ki, ki - qi)
    causal = ahead >= 0
    causal_t = ahead <= 0
    tri = causal.astype(F32).astype(BF16)
    tri_t = causal_t.astype(F32).astype(BF16)
    a_cum = sum(jnp.dot(tri, p, preferred_element_type=F32) for p in _split3(dta))
    a_cum_t = sum(jnp.dot(p, tri_t, preferred_element_type=F32) for p in _split3(dta.T))
    dt_t = dt.T
    total = jnp.sum(dta, axis=0, keepdims=True)
    lane_lo = lax.broadcasted_iota(I32, (L, HEAD_WIDTH), 1) < SSD_HEAD_DIM
    row_lo = lax.broadcasted_iota(I32, (HEAD_WIDTH, SSD_STATE), 0) < SSD_HEAD_DIM

    n_groups = n_heads // heads_per_group
    for g in range(n_groups):
        gs = slice(g * SSD_STATE, (g + 1) * SSD_STATE)
        b_g = bm_ref[:, gs].astype(BF16)
        c_g = cm_ref[:, gs].astype(BF16)
        cb = _nt_dot(c_g, b_g)
        for j in range(heads_per_group // 2):
            pair = (g * heads_per_group) // 2 + j
            ps = slice(pair * HEAD_WIDTH, (pair + 1) * HEAD_WIDTH)
            x_pair = xs_ref[:, ps]
            ws, acols, tots = [], [], []
            for hh in (2 * pair, 2 * pair + 1):
                acol = a_cum[:, hh:hh + 1]
                seg = acol - a_cum_t[hh:hh + 1, :]
                decay = jnp.exp(jnp.where(causal, seg, NEG_INF))
                ws.append(cb * decay * dt_t[hh:hh + 1, :])
                acols.append(acol)
                tots.append(total[:, hh:hh + 1])
            w2 = jnp.concatenate(ws, axis=1).astype(BF16)
            x2 = jnp.concatenate([jnp.where(lane_lo, x_pair, 0.0), jnp.where(lane_lo, 0.0, x_pair)],
                                 axis=0).astype(BF16)
            y_diag = jnp.dot(w2, x2, preferred_element_type=F32)
            h_in = state[ps, :]
            y_off = _nt_dot(c_g, h_in.astype(BF16)) * jnp.where(lane_lo, jnp.exp(acols[0]), jnp.exp(acols[1]))
            y_ref[:, ps] = y_diag + y_off
            to_end = [jnp.exp(tots[i] - acols[i]) * dt[:, 2 * pair + i: 2 * pair + i + 1] for i in range(2)]
            xw = x_pair * jnp.where(lane_lo, to_end[0], to_end[1])
            st = _tn_dot(xw.astype(BF16), b_g)
            dec = jnp.where(row_lo, jnp.exp(tots[0]), jnp.exp(tots[1]))
            state[ps, :] = h_in * dec + st

    if want_state:
        @pl.when(c == pl.num_programs(2) - 1)
        def _():
            hl_ref[...] = state[...]


def _ssd_scan(xbc, dt_raw, dt_bias, a_log, h0, row0_dt, batch, seq, n_heads, want_state):
    L = SSD_CHUNK
    nc = seq // L
    width = n_heads * SSD_HEAD_DIM
    n_groups = SSD_GROUPS
    bc = n_groups * SSD_STATE
    assert width % bc == 0 and row0_dt % L == 0
    r0 = row0_dt // L

    def rb(b, d, c):
        return b * nc + c + d * (nc - 1 - 2 * c)

    in_specs = [
        pl.BlockSpec((L, width), lambda b, d, c: (rb(b, d, c), 0)),
        pl.BlockSpec((L, bc), lambda b, d, c: (rb(b, d, c), width // bc)),
        pl.BlockSpec((L, bc), lambda b, d, c: (rb(b, d, c), width // bc + 1)),
        pl.BlockSpec((L, LANES), lambda b, d, c: (r0 + rb(b, d, c), d)),
        pl.BlockSpec((None, 1, LANES), lambda b, d, c: (d, 0, 0)),
        pl.BlockSpec((None, 1, LANES), lambda b, d, c: (d, 0, 0)),
    ]
    args = [xbc, xbc, xbc, dt_raw, dt_bias, a_log]
    has_h0 = h0 is not None
    if has_h0:
        in_specs.append(pl.BlockSpec((None, None, width, SSD_STATE), lambda b, d, c: (b, d, 0, 0)))
        args.append(h0)
    out_specs = [pl.BlockSpec((None, L, width), lambda b, d, c: (d, rb(b, d, c), 0))]
    out_shape = [jax.ShapeDtypeStruct((2, batch * seq, width), F32)]
    if want_state:
        out_specs.append(pl.BlockSpec((None, None, width, SSD_STATE), lambda b, d, c: (b, d, 0, 0)))
        out_shape.append(jax.ShapeDtypeStruct((batch, 2, width, SSD_STATE), F32))
    return pl.pallas_call(
        functools.partial(_ssd_kernel, n_heads=n_heads, heads_per_group=n_heads // n_groups,
                          has_h0=has_h0, want_state=want_state),
        grid=(batch, 2, nc),
        in_specs=in_specs,
        out_specs=out_specs,
        out_shape=out_shape,
        scratch_shapes=[pltpu.VMEM((width, SSD_STATE), F32)],
        compiler_params=_cparams(("arbitrary", "arbitrary", "arbitrary")),
        name="ssd_scan",
    )(*args)


def _ssd_gate_kernel(y_ref, xs_ref, z_ref, d_ref, nw_ref, o_ref):
    y = y_ref[0] + y_ref[1] + d_ref[...] * xs_ref[...]
    y = y * _silu(z_ref[...])
    o_ref[...] = _rms(y, nw_ref[...]).astype(o_ref.dtype)


def _ssd_gate(y2, xbc, proj, z_col0, row0, d_full, norm_w, width):
    n_rows = y2.shape[1]
    gw = width // SSD_GROUPS
    tm = _pick(n_rows, (512, 256, 128))
    assert z_col0 % gw == 0 and row0 % tm == 0
    zc, r0 = z_col0 // gw, row0 // tm
    return pl.pallas_call(
        _ssd_gate_kernel,
        grid=(n_rows // tm, SSD_GROUPS),
        in_specs=[
            pl.BlockSpec((2, tm, gw), lambda i, g: (0, i, g)),
            pl.BlockSpec((tm, gw), lambda i, g: (i, g)),
            pl.BlockSpec((tm, gw), lambda i, g: (r0 + i, zc + g)),
            pl.BlockSpec((1, gw), lambda i, g: (0, g)),
            pl.BlockSpec((1, gw), lambda i, g: (0, g)),
        ],
        out_specs=pl.BlockSpec((tm, gw), lambda i, g: (i, g)),
        out_shape=jax.ShapeDtypeStruct((n_rows, width), BF16),
        compiler_params=_cparams(("arbitrary", "arbitrary")),
        name="ssd_gate",
    )(y2, xbc, proj, d_full, norm_w)


def _outproj_kernel(d_ref, g_ref, s_ref, wd_ref, wg_ref, ws_ref, o_ref):
    acc = jnp.dot(d_ref[...], wd_ref[...], preferred_element_type=F32)
    acc += jnp.dot(g_ref[...], wg_ref[...], preferred_element_type=F32)
    acc += jnp.dot(s_ref[...], ws_ref[...], preferred_element_type=F32)
    o_ref[...] = acc


def _outproj(d_out, g_out, s_out, w_out):
    t = d_out.shape[0]
    dw, gw, sw = d_out.shape[1], g_out.shape[1], s_out.shape[1]
    d = w_out.shape[1]
    assert dw == gw and sw % dw == 0 and (dw + gw) % sw == 0
    tm = _pick(t, (512, 256, 128))
    tn = _pick(d, (1024, 512, 256, 128))
    return pl.pallas_call(
        _outproj_kernel,
        grid=(t // tm, d // tn),
        in_specs=[
            pl.BlockSpec((tm, dw), lambda i, j: (i, 0)),
            pl.BlockSpec((tm, gw), lambda i, j: (i, 0)),
            pl.BlockSpec((tm, sw), lambda i, j: (i, 0)),
            pl.BlockSpec((dw, tn), lambda i, j: (0, j)),
            pl.BlockSpec((gw, tn), lambda i, j: (1, j)),
            pl.BlockSpec((sw, tn), lambda i, j: ((dw + gw) // sw, j)),
        ],
        out_specs=pl.BlockSpec((tm, tn), lambda i, j: (i, j)),
        out_shape=jax.ShapeDtypeStruct((t, d), F32),
        compiler_params=_cparams(("arbitrary", "arbitrary")),
        name="out_proj",
    )(d_out, g_out, s_out, w_out, w_out, w_out)


def _first_index(hit, idx, sentinel):
    return jnp.min(jnp.where(hit, idx, sentinel), axis=0, keepdims=True)


def _router_kernel(u_ref, x_ref, mod_ref, g_ref, rw_ref, rb_ref,
                   x1_ref, h2_ref, idx_ref, wts_ref, rank_ref, cnt_ref, carry):
    per_group = N_EXPERTS // N_EXPERT_GROUPS

    @pl.when(pl.program_id(0) == 0)
    def _():
        carry[...] = jnp.zeros_like(carry)

    x1 = x_ref[...] + mod_ref[2:3, :] * _rms(u_ref[...], g_ref[1:2, :])
    x1_ref[...] = x1
    h2 = _rms(x1, g_ref[2:3, :]) * (1.0 + mod_ref[4:5, :]) + mod_ref[3:4, :]
    slab, pitch = _slab_pitch(h2.shape[1])
    _slab_store(h2_ref, _pack_rows(h2), slab, pitch)
    logits =_nt_dot(rw_ref[...], h2.astype(BF16))
    scores = 1.0 / (1.0 + jnp.exp(-logits))
    sel = scores + rb_ref[...]
    tm = sel.shape[1]
    sub = lax.broadcasted_iota(I32, (per_group, tm), 0)

    sel_g = [sel[g * per_group:(g + 1) * per_group, :] for g in range(N_EXPERT_GROUPS)]
    sc_g = [scores[g * per_group:(g + 1) * per_group, :] for g in range(N_EXPERT_GROUPS)]
    gscore = jnp.zeros((N_EXPERT_GROUPS, tm), F32)
    gsub = lax.broadcasted_iota(I32, (N_EXPERT_GROUPS, tm), 0)
    for g in range(N_EXPERT_GROUPS):
        v = sel_g[g]
        m1 = jnp.max(v, axis=0, keepdims=True)
        i1 = _first_index(v == m1, sub, per_group)
        m2 = jnp.max(jnp.where(sub == i1, NEG_INF, v), axis=0, keepdims=True)
        gscore = jnp.where(gsub == g, m1 + m2, gscore)
    gmask = jnp.zeros((N_EXPERT_GROUPS, tm), F32)
    gv = gscore
    for _ in range(TOPK_GROUPS):
        m = jnp.max(gv, axis=0, keepdims=True)
        hit = gsub == _first_index(gv == m, gsub, N_EXPERT_GROUPS)
        gmask = jnp.where(hit, 1.0, gmask)
        gv = jnp.where(hit, NEG_INF, gv)
    vals = [jnp.where(gmask[g:g + 1, :] > 0.0, sel_g[g], NEG_INF) for g in range(N_EXPERT_GROUPS)]
    eid = [sub + g * per_group for g in range(N_EXPERT_GROUPS)]

    picked = [jnp.zeros((per_group, tm), F32) for _ in range(N_EXPERT_GROUPS)]
    w_rows, idx_rows = [], []
    for _ in range(TOP_K):
        m = functools.reduce(jnp.maximum, [jnp.max(v, axis=0, keepdims=True) for v in vals])
        first = functools.reduce(jnp.minimum,
                                 [_first_index(v == m, e, N_EXPERTS) for v, e in zip(vals, eid)])
        w = jnp.zeros((1, tm), F32)
        for g in range(N_EXPERT_GROUPS):
            hit = eid[g] == first
            w = w + jnp.sum(jnp.where(hit, sc_g[g], 0.0), axis=0, keepdims=True)
            vals[g] = jnp.where(hit, NEG_INF, vals[g])
            picked[g] = jnp.where(hit, 1.0, picked[g])
        w_rows.append(w)
        idx_rows.append(first)
    wsum = functools.reduce(lambda a, b: a + b, w_rows)

    onehot = jnp.concatenate(picked, axis=0)
    ti = lax.broadcasted_iota(I32, (tm, tm), 0)
    tj = lax.broadcasted_iota(I32, (tm, tm), 1)
    upper = (ti < tj).astype(F32).astype(BF16)
    rank = jnp.dot(onehot.astype(BF16), upper, preferred_element_type=F32) + carry[:, 0:1]
    new_carry = carry[...] + jnp.sum(onehot, axis=1, keepdims=True)
    carry[...] = new_carry
    cnt_ref[...] = new_carry
    for k in range(TOP_K):
        idx_ref[k:k + 1, :] = idx_rows[k]
        wts_ref[k:k + 1, :] = w_rows[k] / wsum * ROUTED_SCALE
        r = jnp.zeros((1, tm), F32)
        for g in range(N_EXPERT_GROUPS):
            r = r + jnp.sum(jnp.where(eid[g] == idx_rows[k], rank[g * per_group:(g + 1) * per_group, :], 0.0),
                            axis=0, keepdims=True)
        rank_ref[k:k + 1, :] = r.astype(I32)


def _router(u, x, mod_l, g4, rw_t, rbias, toks):
    t, d = x.shape
    tm = min(256, _pick(toks.n_ctx, (256, 128)), _pick(toks.n_lat_seq, (256, 128)))
    _, pitch = _slab_pitch(d)
    row = lambda i: (i, 0)
    col = lambda i: (0, i)
    fixed = lambda i: (0, 0)
    return pl.pallas_call(
        _router_kernel,
        grid=(t // tm,),
        in_specs=[
            pl.BlockSpec((tm, d), row),
            pl.BlockSpec((tm, d), row),
            pl.BlockSpec((None, N_MOD, d), lambda i: (toks.mod_row(i, tm), 0, 0)),
            pl.BlockSpec((4, d), fixed),
            pl.BlockSpec((N_EXPERTS, d), fixed),
            pl.BlockSpec((N_EXPERTS, 1), fixed),
        ],
        out_specs=[
            pl.BlockSpec((tm, d), row),
            pl.BlockSpec((tm * pitch, LANES), row),
            pl.BlockSpec((TOP_K, tm), col),
            pl.BlockSpec((TOP_K, tm), col),
            pl.BlockSpec((TOP_K, tm), col),
            pl.BlockSpec((N_EXPERTS, LANES), fixed),
        ],
        out_shape=[
            jax.ShapeDtypeStruct((t, d), F32),
            jax.ShapeDtypeStruct((t * pitch, LANES), I32),
            jax.ShapeDtypeStruct((TOP_K, t), I32),
            jax.ShapeDtypeStruct((TOP_K, t), F32),
            jax.ShapeDtypeStruct((TOP_K, t), I32),
            jax.ShapeDtypeStruct((N_EXPERTS, LANES), F32),
        ],
        scratch_shapes=[pltpu.VMEM((N_EXPERTS, LANES), F32)],
        compiler_params=_cparams(("arbitrary",)),
        name="router",
    )(u, x, mod_l, g4, rw_t, rbias)


def _dispatch_kernel(pstart_ref, fill_ref, idx_ref, rank_ref, h_ref, hs_ref, zbuf, sem, *, pitch):
    tchunk = idx_ref.shape[1]
    blk = zbuf.shape[0]

    def row_copy(t, d):
        src = h_ref.at[pl.ds(pl.multiple_of(t * pitch, pitch), pitch)]
        dst = hs_ref.at[pl.ds(pl.multiple_of(d * pitch, pitch), pitch)]
        return pltpu.make_async_copy(src, dst, sem.at[0])

    @pl.when(pl.program_id(0) == 0)
    def _():
        zbuf[...] = jnp.zeros_like(zbuf)

        def fill_copy(e):
            start = pl.multiple_of(fill_ref[e] * pitch, blk)
            return pltpu.make_async_copy(zbuf, hs_ref.at[pl.ds(start, blk)], sem.at[1])

        def start(e, carry):
            @pl.when(fill_ref[e] >= 0)
            def _():
                fill_copy(e).start()
            return carry

        def wait(e, carry):
            @pl.when(fill_ref[e] >= 0)
            def _():
                fill_copy(e).wait()
            return carry

        lax.fori_loop(0, N_EXPERTS, start, 0)
        lax.fori_loop(0, N_EXPERTS, wait, 0)

    def issue(t, carry):
        for k in range(TOP_K):
            row_copy(t, pstart_ref[idx_ref[k, t]] + rank_ref[k, t]).start()
        return carry

    def drain(t, carry):
        for k in range(TOP_K):
            row_copy(0, 0).wait()
        return carry

    lax.fori_loop(0, tchunk, issue, 0)
    lax.fori_loop(0, tchunk, drain, 0)


def _dispatch(h2p, idx, rank, pad_start, fill_start, m_rows, pitch):
    t = h2p.shape[0] // pitch
    tchunk = _pick(t, (512, 256, 128))
    smem_blk = pl.BlockSpec((TOP_K, tchunk), lambda i, ps, fs: (0, i), memory_space=pltpu.SMEM)
    return pl.pallas_call(
        functools.partial(_dispatch_kernel, pitch=pitch),
        grid_spec=pltpu.PrefetchScalarGridSpec(
            num_scalar_prefetch=2,
            grid=(t // tchunk,),
            in_specs=[
                smem_blk,
                smem_blk,
                pl.BlockSpec((tchunk * pitch, LANES), lambda i, ps, fs: (i, 0)),
            ],
            out_specs=pl.BlockSpec(memory_space=pl.ANY),
            scratch_shapes=[pltpu.VMEM((EXPERT_ROWS * pitch, LANES), I32), pltpu.SemaphoreType.DMA((2,))],
        ),
        out_shape=jax.ShapeDtypeStruct((m_rows * pitch, LANES), I32),
        compiler_params=_cparams(("arbitrary",)),
        name="moe_dispatch",
    )(pad_start, fill_start, idx, rank, h2p)


def _swiglu_packed(h_ref, n, wg_ref, wu_ref, wd_ref):
    d = wg_ref.shape[0]
    slab, pitch = _slab_pitch(d)
    lo, hi = _unpack_rows(_slab_load(h_ref, n, slab, pitch))
    lo, hi = lo.astype(BF16), hi.astype(BF16)
    half = d // 2
    a = (jnp.dot(lo, wg_ref[:half, :], preferred_element_type=F32)
         + jnp.dot(hi, wg_ref[half:, :], preferred_element_type=F32))
    b = (jnp.dot(lo, wu_ref[:half, :], preferred_element_type=F32)
         + jnp.dot(hi, wu_ref[half:, :], preferred_element_type=F32))
    return jnp.dot((_silu(a) * b).astype(BF16), wd_ref[...], preferred_element_type=F32)


def _expert_kernel(blk_e_ref, n_used_ref, hs_ref, wg_ref, wu_ref, wd_ref, y_ref):
    @pl.when(pl.program_id(0) < n_used_ref[0])
    def _():
        y = _swiglu_packed(hs_ref, EXPERT_ROWS, wg_ref, wu_ref, wd_ref)
        slab, pitch = _slab_pitch(y.shape[1])
        _slab_store(y_ref, _pack_rows(y), slab, pitch)


def _experts(hs, blk_e, n_used, wg, wu, wd):
    d, ff = wg.shape[1], wg.shape[2]
    _, pitch = _slab_pitch(d)
    m_rows = hs.shape[0] // pitch
    n_blk = m_rows // EXPERT_ROWS

    def blk(i, blk_e, n_used):
        return jnp.minimum(i, n_used[0] - 1)

    return pl.pallas_call(
        _expert_kernel,
        grid_spec=pltpu.PrefetchScalarGridSpec(
            num_scalar_prefetch=2,
            grid=(n_blk,),
            in_specs=[
                pl.BlockSpec((EXPERT_ROWS * pitch, LANES), lambda i, be, nu: (blk(i, be, nu), 0)),
                pl.BlockSpec((None, d, ff), lambda i, be, nu: (be[blk(i, be, nu)], 0, 0)),
                pl.BlockSpec((None, d, ff), lambda i, be, nu: (be[blk(i, be, nu)], 0, 0)),
                pl.BlockSpec((None, ff, d), lambda i, be, nu: (be[blk(i, be, nu)], 0, 0)),
            ],
            out_specs=pl.BlockSpec((EXPERT_ROWS * pitch, LANES), lambda i, be, nu: (blk(i, be, nu), 0)),
        ),
        out_shape=jax.ShapeDtypeStruct((m_rows * pitch, LANES), I32),
        compiler_params=_cparams(("arbitrary",)),
        name="moe_experts",
    )(blk_e, n_used, hs, wg, wu, wd)


def _shared_kernel(h_ref, wg_ref, wu_ref, wd_ref, o_ref):
    o_ref[...] = _swiglu_packed(h_ref, o_ref.shape[0], wg_ref, wu_ref, wd_ref)


def _shared_expert(h2p, wg, wu, wd):
    d, ff = wg.shape
    _, pitch = _slab_pitch(d)
    t = h2p.shape[0] // pitch
    tm = _pick(t, (256, 128))
    fixed = lambda i: (0, 0)
    return pl.pallas_call(
        _shared_kernel,
        grid=(t // tm,),
        in_specs=[
            pl.BlockSpec((tm * pitch, LANES), lambda i: (i, 0)),
            pl.BlockSpec((d, ff), fixed),
            pl.BlockSpec((d, ff), fixed),
            pl.BlockSpec((ff, d), fixed),
        ],
        out_specs=pl.BlockSpec((tm, d), lambda i: (i, 0)),
        out_shape=jax.ShapeDtypeStruct((t, d), F32),
        compiler_params=_cparams(("arbitrary",)),
        name="shared_expert",
    )(h2p, wg, wu, wd)


def _combine_kernel(pstart_ref, idx_ref, rank_ref, y_ref, w_ref, sh_ref, x1_ref, mod_ref, g_ref, o_ref, ybuf, sem):
    tm, d_model = x1_ref.shape
    slab, pitch = _slab_pitch(d_model)
    half = d_model // 2

    def row_copy(k, t, d):
        src = y_ref.at[pl.ds(pl.multiple_of(d * pitch, pitch), pitch)]
        dst = ybuf.at[k, pl.ds(pl.multiple_of(t * pitch, pitch), pitch)]
        return pltpu.make_async_copy(src, dst, sem.at[0])

    def issue(t, carry):
        for k in range(TOP_K):
            row_copy(k, t, pstart_ref[idx_ref[k, t]] + rank_ref[k, t]).start()
        return carry

    def drain(t, carry):
        for k in range(TOP_K):
            row_copy(0, 0, 0).wait()
        return carry

    lax.fori_loop(0, tm, issue, 0)
    lax.fori_loop(0, tm, drain, 0)
    f_lo = sh_ref[:, :half]
    f_hi = sh_ref[:, half:]
    for k in range(TOP_K):
        lo, hi = _unpack_rows(_slab_load(ybuf, tm, slab, pitch, lead=(k,)))
        w = w_ref[:, k:k + 1]
        f_lo = f_lo + lo * w
        f_hi = f_hi + hi * w
    f = jnp.concatenate([f_lo, f_hi], axis=1)
    o_ref[...] = x1_ref[...] + mod_ref[5:6, :] * _rms(f, g_ref[3:4, :])


def _combine(y, idx, rank, pad_start, wts_t, sh, x1, mod_l, g4, toks):
    t, d = x1.shape
    _, pitch = _slab_pitch(d)
    tm = 128
    smem_blk = pl.BlockSpec((TOP_K, tm), lambda i, ps: (0, i), memory_space=pltpu.SMEM)
    return pl.pallas_call(
        _combine_kernel,
        grid_spec=pltpu.PrefetchScalarGridSpec(
            num_scalar_prefetch=1,
            grid=(t // tm,),
            in_specs=[
                smem_blk,
                smem_blk,
                pl.BlockSpec(memory_space=pl.ANY),
                pl.BlockSpec((tm, TOP_K), lambda i, ps: (i, 0)),
                pl.BlockSpec((tm, d), lambda i, ps: (i, 0)),
                pl.BlockSpec((tm, d), lambda i, ps: (i, 0)),
                pl.BlockSpec((None, N_MOD, d), lambda i, ps: (toks.mod_row(i, tm), 0, 0)),
                pl.BlockSpec((4, d), lambda i, ps: (0, 0)),
            ],
            out_specs=pl.BlockSpec((tm, d), lambda i, ps: (i, 0)),
            scratch_shapes=[pltpu.VMEM((TOP_K, tm * pitch, LANES), I32), pltpu.SemaphoreType.DMA((1,))],
        ),
        out_shape=jax.ShapeDtypeStruct((t, d), F32),
        compiler_params=_cparams(("arbitrary",)),
        name="moe_combine",
    )(pad_start, idx, rank, y, wts_t, sh, x1, mod_l, g4)


def _axial_tables(n_tokens, dim):
    rows = n_tokens // GRID_W
    row = jnp.repeat(jnp.arange(rows), GRID_W).astype(F32)
    col = (jnp.arange(rows * GRID_W) % GRID_W).astype(F32)
    n_freq = dim // 4
    inv = jnp.exp(-math.log(ROPE_THETA) * jnp.arange(n_freq, dtype=F32) / n_freq)
    ang = jnp.concatenate([row[:, None] * inv, col[:, None] * inv], axis=-1)
    cos, sin = jnp.cos(ang), jnp.sin(ang)
    reps = HEAD_WIDTH // dim
    cos_full = jnp.tile(jnp.concatenate([cos, cos], axis=-1), (1, reps))
    sin_full = jnp.tile(jnp.concatenate([-sin, sin], axis=-1), (1, reps))
    return cos_full, sin_full


def kernel(x_prompt, x_sample, cache_diff_k, cache_diff_v, cache_gqa_k, cache_gqa_v, state_ssd, c, c_ctx, w_ada, b_ada, norm_g, w_in, w_out, diff_lambda, diff_subnorm, gqa_qk_norm, ssd_conv_w, ssd_conv_b, ssd_dt_bias, ssd_a_log, ssd_d, ssd_norm, router_w, router_bias, exp_w_gate, exp_w_up, exp_w_down, sh_w_gate, sh_w_up, sh_w_down):
    batch, seq, d = x_prompt.shape
    dec_batch, dec_seq, _ = x_sample.shape
    depth = w_ada.shape[0]
    past = cache_diff_k.shape[2]
    n_dh = cache_diff_k.shape[3]
    n_kv = cache_gqa_k.shape[3]
    n_gh = (w_out.shape[1] - n_dh * HEAD_WIDTH - ssd_norm.shape[1]) // HEAD_WIDTH
    ssd_w = ssd_norm.shape[1]
    n_sh = ssd_w // SSD_HEAD_DIM
    bc = SSD_GROUPS * SSD_STATE
    dw, gw, kw = n_dh * HEAD_WIDTH, n_gh * HEAD_WIDTH, n_kv * HEAD_WIDTH
    assert 1 + dec_batch <= 8 and n_sh <= LANES

    t_ctx, t_lat = batch * seq, dec_batch * dec_seq
    t_all = t_ctx + t_lat
    toks = _Tokens(t_ctx, dec_seq)
    x = jnp.concatenate([x_prompt.reshape(t_ctx, d), x_sample.reshape(t_lat, d)], axis=0)

    cond = jnp.zeros((8, d), F32).at[0].set(c_ctx).at[1:1 + dec_batch].set(c)
    mod = _ada(cond, w_ada, b_ada).reshape(depth, 8, N_MOD, d)

    z_col = 3 * dw + gw + 2 * kw
    xbc_col = z_col + ssd_w
    dt_col = xbc_col + ssd_w + 2 * bc
    rope_d = _axial_tables(dec_seq, HEAD_WIDTH // 2)
    rope_g = _axial_tables(dec_seq, HEAD_WIDTH)

    tk_all = t_all * TOP_K
    n_blk = tk_all // EXPERT_ROWS + N_EXPERTS
    m_rows = n_blk * EXPERT_ROWS

    new_dk, new_dv, new_gk, new_gv, new_st = [], [], [], [], []
    for li in range(depth):
        w_in_l = w_in[li]
        w_main = w_in_l[:, :dt_col].astype(BF16)
        w_dt = jnp.zeros((d, 2 * LANES), F32)
        w_dt = w_dt.at[:, :n_sh].set(w_in_l[:, dt_col:dt_col + n_sh])
        w_dt = w_dt.at[:, LANES:LANES + n_sh].set(w_in_l[:, dt_col + n_sh:]).astype(BF16)
        proj, dt_raw = _inproj(x, mod[li], norm_g[li, 0:1], w_main, w_dt, toks)

        pad_heads = lambda v: jnp.zeros((2, 1, LANES), F32).at[:, 0, :n_sh].set(v)
        dt_bias, a_log = pad_heads(ssd_dt_bias[li]), pad_heads(ssd_a_log[li])
        d_full = jnp.repeat(ssd_d[li], SSD_HEAD_DIM)[None, :]
        norm_w = ssd_norm[li][None, :]
        lam_init = 0.8 - 0.6 * math.exp(-0.3 * li)
        diff_args = (diff_lambda[li], diff_subnorm[li][None, :], lam_init)

        qd, kd, vd, qg, kg, vg, kg_f32 = _prep(proj, gqa_qk_norm[li], 0, t_ctx, seq, n_dh, n_gh, n_kv, None)
        as_seq = lambda a, b, n: a.reshape(b, n, a.shape[-1])
        d_ctx = _attention(qd, [(as_seq(kd, batch, seq), as_seq(vd, batch, seq), seq)], n_dh, 1, batch, seq,
                           diff=diff_args)
        g_ctx = _attention(qg, [(as_seq(kg, batch, seq), as_seq(vg, batch, seq), seq)], n_gh, n_gh // n_kv,
                           batch, seq)
        xbc_c = _conv_silu(proj, xbc_col, ssd_w + 2 * bc, ssd_conv_w[li], ssd_conv_b[li][None, :], 0, batch, seq)
        y_c, h_last = _ssd_scan(xbc_c, dt_raw, dt_bias, a_log, None, 0, batch, seq, n_sh, True)
        s_ctx = _ssd_gate(y_c, xbc_c, proj, z_col, 0, d_full, norm_w, ssd_w)
        new_dk.append(proj[:t_ctx, dw:2 * dw].reshape(batch, seq, n_dh, 2, HEAD_WIDTH // 2))
        new_dv.append(proj[:t_ctx, 2 * dw:3 * dw].reshape(batch, seq, n_dh, HEAD_WIDTH))
        new_gk.append(kg_f32.reshape(batch, seq, n_kv, HEAD_WIDTH))
        new_gv.append(proj[:t_ctx, 3 * dw + gw + kw: 3 * dw + gw + 2 * kw].reshape(batch, seq, n_kv, HEAD_WIDTH))
        new_st.append(h_last.reshape(batch, 2, n_sh, SSD_HEAD_DIM, SSD_STATE))

        qd, kd, vd, qg, kg, vg = _prep(proj, gqa_qk_norm[li], t_ctx, t_lat, dec_seq, n_dh, n_gh, n_kv,
                                       rope_d + rope_g)
        c_dk = cache_diff_k[:, li].reshape(dec_batch, past, dw)
        c_dv = cache_diff_v[:, li].reshape(dec_batch, past, dw)
        c_gk = cache_gqa_k[:, li].reshape(dec_batch, past, kw)
        c_gv = cache_gqa_v[:, li].reshape(dec_batch, past, kw)
        d_lat = _attention(qd, [(c_dk, c_dv, past), (as_seq(kd, dec_batch, dec_seq), as_seq(vd, dec_batch, dec_seq), dec_seq)],
                           n_dh, 1, dec_batch, dec_seq, diff=diff_args)
        g_lat = _attention(qg, [(c_gk, c_gv, past), (as_seq(kg, dec_batch, dec_seq), as_seq(vg, dec_batch, dec_seq), dec_seq)],
                           n_gh, n_gh // n_kv, dec_batch, dec_seq)
        xbc_l = _conv_silu(proj, xbc_col, ssd_w + 2 * bc, ssd_conv_w[li], ssd_conv_b[li][None, :], t_ctx,
                           dec_batch, dec_seq)
        h0 = state_ssd[:, li].reshape(dec_batch, 2, ssd_w, SSD_STATE)
        (y_l,) = _ssd_scan(xbc_l, dt_raw, dt_bias, a_log, h0, t_ctx, dec_batch, dec_seq, n_sh, False)
        s_lat = _ssd_gate(y_l, xbc_l, proj, z_col, t_ctx, d_full, norm_w, ssd_w)

        cat = lambda a, b: jnp.concatenate([a, b], axis=0)
        u = _outproj(cat(d_ctx, d_lat), cat(g_ctx, g_lat), cat(s_ctx, s_lat), w_out[li].astype(BF16))
        x1, h2p, idx, wts, rank, counts = _router(u, x, mod[li], norm_g[li], router_w[li].T.astype(BF16),
                                                 router_bias[li][:, None], toks)

        counts = counts[:, 0].astype(I32)
        padded = (counts + EXPERT_ROWS - 1) // EXPERT_ROWS * EXPERT_ROWS
        pad_end = jnp.cumsum(padded)
        pad_start = (pad_end - padded).astype(I32)
        fill_start = jnp.where(counts > 0, pad_end - EXPERT_ROWS, -1).astype(I32)
        n_used = (pad_end[-1:] // EXPERT_ROWS).astype(I32)
        blk_row0 = jnp.arange(n_blk, dtype=I32) * EXPERT_ROWS
        blk_e = jnp.minimum(jnp.sum((pad_end[None, :] <= blk_row0[:, None]).astype(I32), axis=1), N_EXPERTS - 1)

        hs = _dispatch(h2p, idx, rank, pad_start, fill_start, m_rows, _slab_pitch(d)[1])
        y = _experts(hs, blk_e, n_used, exp_w_gate[li].astype(BF16), exp_w_up[li].astype(BF16),
                     exp_w_down[li].astype(BF16))
        sh = _shared_expert(h2p, sh_w_gate[li].astype(BF16), sh_w_up[li].astype(BF16), sh_w_down[li].astype(BF16))
        x = _combine(y, idx, rank, pad_start, wts.T, sh, x1, mod[li], norm_g[li], toks)

    y_prompt = x[:t_ctx].reshape(batch, seq, d)
    y_sample = x[t_ctx:].reshape(dec_batch, dec_seq, d)
    stack = lambda parts: jnp.stack(parts, axis=1)
    return (y_prompt, y_sample, stack(new_dk), stack(new_dv), stack(new_gk), stack(new_gv), stack(new_st))
```

```python
import functools
import math

import jax
import jax.numpy as jnp
from jax import lax
from jax.experimental import pallas as pl
from jax.experimental.pallas import tpu as pltpu

F32 = jnp.float32
BF16 = jnp.bfloat16
I32 = jnp.int32

LANES = 128
HEAD_WIDTH = 128
GRID_W = 64
GQA_KV_HEADS = 2
SSD_HEAD_DIM = 64
SSD_GROUPS = 4
SSD_STATE = 128
SSD_CHUNK = 128
ROPE_THETA = 10000.0
RMS_EPS = 1e-6
N_MOD = 6
N_EXPERTS = 64
TOP_K = 8
N_EXPERT_GROUPS = 8
TOPK_GROUPS = 4
ROUTED_SCALE = 2.5
EXPERT_ROWS = 256
ATTN_TQ = 512
ATTN_TK = 2048
VMEM_LIMIT = 56 * 1024 * 1024
NEG_INF = float("-inf")


def _cparams(sem):
    return pltpu.CompilerParams(dimension_semantics=sem, vmem_limit_bytes=VMEM_LIMIT)


def _pick(n, cands):
    for c in cands:
        if n % c == 0:
            return c
    raise ValueError(f"no tile for {n} in {cands}")


def _silu(x):
    return x * (1.0 / (1.0 + jnp.exp(-x)))


def _rms(x, g):
    return x * lax.rsqrt(jnp.mean(x * x, axis=-1, keepdims=True) + RMS_EPS) * g


def _nt_dot(a, b):
    return lax.dot_general(a, b, (((1,), (1,)), ((), ())), preferred_element_type=F32)


def _tn_dot(a, b):
    return lax.dot_general(a, b, (((0,), (0,)), ((), ())), preferred_element_type=F32)


HI_MASK = -65536


def _slab_pitch(d):
    slab = d // (2 * LANES)
    assert slab % 8 == 0
    return slab, (slab if (slab // 8) % 2 == 1 else slab + 8)


def _pack_rows(v):
    half = v.shape[1] // 2
    lo = lax.bitcast_convert_type(v[:, :half].astype(BF16).astype(F32), I32)
    hi = lax.bitcast_convert_type(v[:, half:].astype(BF16).astype(F32), I32)
    return lax.shift_right_logical(lo, 16) | (hi & HI_MASK)


def _unpack_rows(w):
    lo = lax.bitcast_convert_type(lax.shift_left(w, 16), F32)
    hi = lax.bitcast_convert_type(w & HI_MASK, F32)
    return lo, hi


def _slab_load(ref, n, slab, pitch, lead=()):
    return jnp.concatenate([ref[lead + (pl.ds(a, n, stride=pitch), slice(None))] for a in range(slab)], axis=1)


def _slab_store(ref, words, slab, pitch):
    n = words.shape[0]
    for a in range(slab):
        ref[pl.ds(a, n, stride=pitch), :] = words[:, a * LANES:(a + 1) * LANES]
    for a in range(slab, pitch):
        ref[pl.ds(a, n, stride=pitch), :] = jnp.zeros((n, LANES), I32)


def _ada_kernel(cond_ref, w_ref, b_ref, o_ref):
    s = _silu(cond_ref[...]).astype(BF16)
    o_ref[0] = jnp.dot(s, w_ref[0].astype(BF16), preferred_element_type=F32) + b_ref[0]


def _ada(cond, w_ada, b_ada):
    depth, d, n = w_ada.shape
    tn = _pick(n, (512, 256, 128))
    return pl.pallas_call(
        _ada_kernel,
        grid=(depth, n // tn),
        in_specs=[
            pl.BlockSpec((8, d), lambda l, j: (0, 0)),
            pl.BlockSpec((1, d, tn), lambda l, j: (l, 0, j)),
            pl.BlockSpec((1, 1, tn), lambda l, j: (l, 0, j)),
        ],
        out_specs=pl.BlockSpec((1, 8, tn), lambda l, j: (l, 0, j)),
        out_shape=jax.ShapeDtypeStruct((depth, 8, n), F32),
        compiler_params=_cparams(("arbitrary", "arbitrary")),
        name="ada_mod",
    )(cond, w_ada, b_ada.reshape(depth, 1, n))


class _Tokens:
    def __init__(self, n_ctx, n_lat_seq):
        self.n_ctx = n_ctx
        self.n_lat_seq = n_lat_seq

    def mod_row(self, i, tm):
        assert self.n_ctx % tm == 0 and self.n_lat_seq % tm == 0
        ctx_tiles = self.n_ctx // tm
        per_b = self.n_lat_seq // tm
        return jnp.where(i < ctx_tiles, 0, 1 + (i - ctx_tiles) // per_b)


def _inproj_kernel(x_ref, mod_ref, g_ref, w_ref, wdt_ref, o_ref, odt_ref, h_scr):
    @pl.when(pl.program_id(1) == 0)
    def _():
        h = _rms(x_ref[...], g_ref[...]) * (1.0 + mod_ref[1:2, :]) + mod_ref[0:1, :]
        hb = h.astype(BF16)
        h_scr[...] = hb
        odt_ref[...] = jnp.dot(hb, wdt_ref[...], preferred_element_type=F32)

    o_ref[...] = jnp.dot(h_scr[...], w_ref[...], preferred_element_type=F32)


def _inproj(x, mod_l, g0, w_main, w_dt, toks):
    t, d = x.shape
    n = w_main.shape[1]
    tm = _pick(t, (512, 256, 128))
    tm = min(tm, _pick(toks.n_ctx, (512, 256, 128)), _pick(toks.n_lat_seq, (512, 256, 128)))
    tn = _pick(n, (512, 256, 128))
    ndt = w_dt.shape[1]
    return pl.pallas_call(
        _inproj_kernel,
        grid=(t // tm, n // tn),
        in_specs=[
            pl.BlockSpec((tm, d), lambda i, j: (i, 0)),
            pl.BlockSpec((None, N_MOD, d), lambda i, j: (toks.mod_row(i, tm), 0, 0)),
            pl.BlockSpec((1, d), lambda i, j: (0, 0)),
            pl.BlockSpec((d, tn), lambda i, j: (0, j)),
            pl.BlockSpec((d, ndt), lambda i, j: (0, 0)),
        ],
        out_specs=[
            pl.BlockSpec((tm, tn), lambda i, j: (i, j)),
            pl.BlockSpec((tm, ndt), lambda i, j: (i, 0)),
        ],
        out_shape=[jax.ShapeDtypeStruct((t, n), F32), jax.ShapeDtypeStruct((t, ndt), F32)],
        scratch_shapes=[pltpu.VMEM((tm, d), BF16)],
        compiler_params=_cparams(("arbitrary", "arbitrary")),
        name="in_proj",
    )(x, mod_l, g0, w_main, w_dt)


def _attn_tq(seq):
    return min(ATTN_TQ, seq)


def _attn_tk(seq):
    return min(ATTN_TK, seq)


def _rope128(x, cos, sin, half):
    if half == 64:
        partner = pltpu.roll(x, 64, 1)
    else:
        lane = lax.broadcasted_iota(I32, x.shape, 1)
        partner = jnp.where((lane & (2 * half - 1)) < half, pltpu.roll(x, LANES - half, 1), pltpu.roll(x, half, 1))
    return x * cos + partner * sin


def _prep_kernel(*refs, n_dh, n_gh, n_kv, rope):
    if rope:
        p_ref, qkn_ref, cd_ref, sd_ref, cg_ref, sg_ref, qd_ref, kd_ref, vd_ref, qg_ref, kg_ref, vg_ref = refs
    else:
        p_ref, qkn_ref, qd_ref, kd_ref, vd_ref, qg_ref, kg_ref, vg_ref, kgf_ref = refs
    dw = n_dh * HEAD_WIDTH
    gw = n_gh * HEAD_WIDTH
    kw = n_kv * HEAD_WIDTH
    tn = p_ref.shape[0]
    d_scale = (HEAD_WIDTH // 2) ** -0.5
    g_scale = HEAD_WIDTH ** -0.5

    def put_v(v_ref, h, v):
        sl = slice(h * HEAD_WIDTH, (h + 1) * HEAD_WIDTH)
        if rope:
            v_ref[sl, :] = v.T.astype(BF16)
        else:
            v_ref[:, sl] = v.astype(BF16)

    for h in range(n_dh):
        sl = slice(h * HEAD_WIDTH, (h + 1) * HEAD_WIDTH)
        q = p_ref[:, sl]
        k = p_ref[:, dw + h * HEAD_WIDTH: dw + (h + 1) * HEAD_WIDTH]
        if rope:
            q = _rope128(q, cd_ref[...], sd_ref[...], HEAD_WIDTH // 4)
            k = _rope128(k, cd_ref[...], sd_ref[...], HEAD_WIDTH // 4)
        qd_ref[sl, :] = (q * d_scale).T.astype(BF16)
        kd_ref[:, sl] = k.astype(BF16)
        put_v(vd_ref, h, p_ref[:, 2 * dw + h * HEAD_WIDTH: 2 * dw + (h + 1) * HEAD_WIDTH])
    off = 3 * dw
    for h in range(n_gh):
        sl = slice(h * HEAD_WIDTH, (h + 1) * HEAD_WIDTH)
        q = _rms(p_ref[:, off + h * HEAD_WIDTH: off + (h + 1) * HEAD_WIDTH], qkn_ref[0:1, :])
        if rope:
            q = _rope128(q, cg_ref[...], sg_ref[...], HEAD_WIDTH // 2)
        qg_ref[sl, :] = (q * g_scale).T.astype(BF16)
    off += gw
    for h in range(n_kv):
        sl = slice(h * HEAD_WIDTH, (h + 1) * HEAD_WIDTH)
        k = _rms(p_ref[:, off + h * HEAD_WIDTH: off + (h + 1) * HEAD_WIDTH], qkn_ref[1:2, :])
        if rope:
            k = _rope128(k, cg_ref[...], sg_ref[...], HEAD_WIDTH // 2)
        else:
            kgf_ref[:, sl] = k
        kg_ref[:, sl] = k.astype(BF16)
        put_v(vg_ref, h, p_ref[:, off + kw + h * HEAD_WIDTH: off + kw + (h + 1) * HEAD_WIDTH])


def _prep(proj, qk_norm, row0, n_rows, seq, n_dh, n_gh, n_kv, rope_tabs):
    dw, gw, kw = n_dh * HEAD_WIDTH, n_gh * HEAD_WIDTH, n_kv * HEAD_WIDTH
    width = 3 * dw + gw + 2 * kw
    tn = _attn_tq(seq)
    tk = _attn_tk(seq)
    assert row0 % tn == 0 and seq % tk == 0 and tk % tn == 0
    r0 = row0 // tn
    n_tiles = n_rows // tn
    rope = rope_tabs is not None
    in_specs = [
        pl.BlockSpec((tn, width), lambda i: (r0 + i, 0)),
        pl.BlockSpec((2, HEAD_WIDTH), lambda i: (0, 0)),
    ]
    args = [proj, qk_norm]
    if rope:
        per_seq = seq // tn
        for tab in rope_tabs:
            in_specs.append(pl.BlockSpec((tn, HEAD_WIDTH), lambda i: (i % per_seq, 0)))
            args.append(tab)
    tok = lambda w: (pl.BlockSpec((tn, w), lambda i: (i, 0)), jax.ShapeDtypeStruct((n_rows, w), BF16))
    q_t = lambda w: (pl.BlockSpec((None, w, tn), lambda i: (i, 0, 0)), jax.ShapeDtypeStruct((n_tiles, w, tn), BF16))
    tpc = tk // tn
    v_t = lambda w: (pl.BlockSpec((None, w, tn), lambda i: (i // tpc, 0, i % tpc)),
                     jax.ShapeDtypeStruct((n_rows // tk, w, tk), BF16))
    v_out = v_t if rope else tok
    outs = [q_t(dw), tok(dw), v_out(dw), q_t(gw), tok(kw), v_out(kw)]
    if not rope:
        outs.append((pl.BlockSpec((tn, kw), lambda i: (i, 0)), jax.ShapeDtypeStruct((n_rows, kw), F32)))
    return pl.pallas_call(
        functools.partial(_prep_kernel, n_dh=n_dh, n_gh=n_gh, n_kv=n_kv, rope=rope),
        grid=(n_tiles,),
        in_specs=in_specs,
        out_specs=[o[0] for o in outs],
        out_shape=[o[1] for o in outs],
        compiler_params=_cparams(("arbitrary",)),
        name="attn_prep_rope" if rope else "attn_prep",
    )(*args)


def _softmax_step(state, s, pv):
    m, l, acc = state
    m_new = jnp.maximum(m, jnp.max(s, axis=0, keepdims=True))
    alpha = jnp.exp(m - m_new)
    p = jnp.exp(s - m_new)
    l = alpha * l + jnp.sum(p, axis=0, keepdims=True)
    acc = alpha * acc + pv(p.astype(BF16))
    return m_new, l, acc


def _attn_kernel(*refs, segs, diff, lam_init):
    n_seg = len(segs)
    q_ref = refs[0]
    kv_refs = refs[1: 1 + 2 * n_seg]
    s_scr = p_scr = None
    if any(v_t for _, _, v_t in segs):
        refs, (s_scr, p_scr) = refs[:-2], refs[-2:]
    if diff:
        lam_ref, sub_ref, o_ref = refs[1 + 2 * n_seg:]
    else:
        (o_ref,) = refs[1 + 2 * n_seg:]
    qt = q_ref[...]
    tq = qt.shape[1]
    if diff:
        row = lax.broadcasted_iota(I32, qt.shape, 0)
        qs = [jnp.where(row < HEAD_WIDTH // 2, qt, jnp.zeros_like(qt)),
              jnp.where(row < HEAD_WIDTH // 2, jnp.zeros_like(qt), qt)]
    else:
        qs = [qt]
    init = (jnp.full((1, tq), NEG_INF, F32), jnp.zeros((1, tq), F32), jnp.zeros((HEAD_WIDTH, tq), F32))
    states = tuple(init for _ in qs)

    for si, (t_len, tk, v_t) in enumerate(segs):
        k_ref, v_ref = kv_refs[2 * si], kv_refs[2 * si + 1]

        def chunk(c, sts, k_ref=k_ref, v_ref=v_ref, tk=tk, v_t=v_t):
            start = pl.multiple_of(c * tk, tk)
            k = k_ref[pl.ds(start, tk), :].astype(BF16)
            if v_t:
                vt = v_ref[c]
                pv = lambda p: jnp.dot(vt, p, preferred_element_type=F32)
            else:
                v = v_ref[pl.ds(start, tk), :].astype(BF16)
                pv = lambda p: _tn_dot(v, p)
            return tuple(_softmax_step(st, jnp.dot(k, qm, preferred_element_type=F32), pv)
                         for st, qm in zip(sts, qs))

        n_chunks = t_len // tk
        if n_chunks == 1:
            states = chunk(0, states)
        elif not v_t:
            states = lax.fori_loop(0, n_chunks, chunk, states)
        else:
            for c in range(n_chunks):
                slot = c % 2
                k = k_ref[c * tk:(c + 1) * tk, :]
                for i, qm in enumerate(qs):
                    s_scr[slot, i] = jnp.dot(k, qm, preferred_element_type=F32)
                new_states = []
                for i, (m, l, acc) in enumerate(states):
                    m_new = jnp.maximum(m, jnp.max(s_scr[slot, i], axis=0, keepdims=True))
                    alpha = jnp.exp(m - m_new)
                    p = jnp.exp(s_scr[slot, i] - m_new)
                    l = alpha * l + jnp.sum(p, axis=0, keepdims=True)
                    p_scr[slot, i] = p.astype(BF16)
                    acc = alpha * acc + jnp.dot(v_ref[c], p_scr[slot, i], preferred_element_type=F32)
                    new_states.append((m_new, l, acc))
                states = tuple(new_states)

    outs = [acc / l for (_, l, acc) in states]
    if diff:
        lp = lam_ref[...]
        lam = (jnp.exp(jnp.sum(lp[0:1] * lp[1:2], axis=-1, keepdims=True))
               - jnp.exp(jnp.sum(lp[2:3] * lp[3:4], axis=-1, keepdims=True)) + lam_init)
        o = outs[0] - lam * outs[1]
        o = o * lax.rsqrt(jnp.mean(o * o, axis=0, keepdims=True) + RMS_EPS) * (1.0 - lam_init)
        o = o.T * sub_ref[...]
    else:
        o = outs[0].T
    o_ref[...] = o.astype(o_ref.dtype)


def _attention(qt, kvs, n_heads, kv_group, batch, seq, diff=None):
    tq = _attn_tq(seq)
    per_seq = seq // tq
    in_specs = [pl.BlockSpec((None, HEAD_WIDTH, tq), lambda b, h, i: (b * per_seq + i, h, 0))]
    args = [qt]
    segs = []
    tk_t = 0
    for k, v, t_len, v_t in kvs:
        in_specs.append(pl.BlockSpec((None, t_len, HEAD_WIDTH), lambda b, h, i: (b, 0, h // kv_group)))
        if v_t:
            tk = tk_t = _attn_tk(t_len)
            in_specs.append(pl.BlockSpec((t_len // tk, HEAD_WIDTH, tk), lambda b, h, i: (b, h // kv_group, 0)))
        else:
            tk = _pick(t_len, (256, 128))
            in_specs.append(pl.BlockSpec((None, t_len, HEAD_WIDTH), lambda b, h, i: (b, 0, h // kv_group)))
        args += [k, v]
        segs.append((t_len, tk, v_t))
    lam_init = 0.0
    if diff is not None:
        lam_params, subnorm, lam_init = diff
        in_specs.append(pl.BlockSpec(lam_params.shape, lambda b, h, i: (0, 0)))
        in_specs.append(pl.BlockSpec((1, HEAD_WIDTH), lambda b, h, i: (0, 0)))
        args += [lam_params, subnorm]
    scratch = []
    if tk_t:
        n_maps = 2 if diff is not None else 1
        scratch = [pltpu.VMEM((2, n_maps, tk_t, tq), F32), pltpu.VMEM((2, n_maps, tk_t, tq), BF16)]
    return pl.pallas_call(
        functools.partial(_attn_kernel, segs=tuple(segs), diff=diff is not None, lam_init=lam_init),
        grid=(batch, n_heads, per_seq),
        scratch_shapes=scratch,
        in_specs=in_specs,
        out_specs=pl.BlockSpec((tq, HEAD_WIDTH), lambda b, h, i: (b * per_seq + i, h)),
        out_shape=jax.ShapeDtypeStruct((batch * seq, n_heads * HEAD_WIDTH), BF16),
        compiler_params=_cparams(("arbitrary", "arbitrary", "arbitrary")),
        name="diff_attn" if diff is not None else "gqa_attn",
    )(*args)


def _conv_kernel(u_ref, w_ref, b_ref, o_ref):
    u = u_ref[...]
    n = u.shape[0]
    row = lax.broadcasted_iota(I32, u.shape, 0)
    prev = jnp.where(row == 0, 0.0, pltpu.roll(u, 1, 0))
    nxt = jnp.where(row == n - 1, 0.0, pltpu.roll(u, n - 1, 0))
    y = w_ref[0:1, :] * prev + w_ref[1:2, :] * u + w_ref[2:3, :] * nxt + b_ref[...]
    o_ref[...] = _silu(y)


def _conv_silu(proj, col0, width, conv_w, conv_b, row0, batch, seq):
    tc = 256
    assert col0 % tc == 0 and width % tc == 0 and row0 % seq == 0
    c0, r0 = col0 // tc, row0 // seq
    return pl.pallas_call(
        _conv_kernel,
        grid=(batch, width // tc),
        in_specs=[
            pl.BlockSpec((seq, tc), lambda b, j: (r0 + b, c0 + j)),
            pl.BlockSpec((conv_w.shape[0], tc), lambda b, j: (0, j)),
            pl.BlockSpec((1, tc), lambda b, j: (0, j)),
        ],
        out_specs=pl.BlockSpec((seq, tc), lambda b, j: (b, j)),
        out_shape=jax.ShapeDtypeStruct((batch * seq, width), F32),
        compiler_params=_cparams(("arbitrary", "arbitrary")),
        name="ssd_conv",
    )(proj, conv_w, conv_b)


def _split3(x):
    hi = x.astype(BF16)
    r = x - hi.astype(F32)
    mid = r.astype(BF16)
    lo = (r - mid.astype(F32)).astype(BF16)
    return hi, mid, lo


def _ssd_kernel(*refs, n_heads, heads_per_group, has_h0, want_state):
    it = iter(refs)
    xs_ref, bm_ref, cm_ref, dt_ref, dtb_ref, alog_ref = (next(it) for _ in range(6))
    h0_ref = next(it) if has_h0 else None
    y_ref = next(it)
    hl_ref = next(it) if want_state else None
    state = next(it)

    fwd = pl.program_id(1) == 0
    c = pl.program_id(2)
    L = SSD_CHUNK

    @pl.when(c == 0)
    def _():
        if has_h0:
            state[...] = h0_ref[...]
        else:
            state[...] = jnp.zeros_like(state)

    z = dt_ref[...] + dtb_ref[...]
    dt = jnp.maximum(z, 0.0) + jnp.log1p(jnp.exp(-jnp.abs(z)))
    dta = dt * (-jnp.exp(alog_ref[...]))
    qi = lax.broadcasted_iota(I32, (L, L), 0)
    ki = lax.broadcasted_iota(I32, (L, L), 1)
    ahead = jnp.where(fwd, qi - ki, ki - qi)
    causal = ahead >= 0
    causal_t = ahead <= 0
    tri = causal.astype(F32).astype(BF16)
    tri_t = causal_t.astype(F32).astype(BF16)
    a_cum = sum(jnp.dot(tri, p, preferred_element_type=F32) for p in _split3(dta))
    a_cum_t = sum(jnp.dot(p, tri_t, preferred_element_type=F32) for p in _split3(dta.T))
    dt_t = dt.T
    total = jnp.sum(dta, axis=0, keepdims=True)
    lane_lo = lax.broadcasted_iota(I32, (L, HEAD_WIDTH), 1) < SSD_HEAD_DIM
    row_lo = lax.broadcasted_iota(I32, (HEAD_WIDTH, SSD_STATE), 0) < SSD_HEAD_DIM

    n_groups = n_heads // heads_per_group
    for g in range(n_groups):
        gs = slice(g * SSD_STATE, (g + 1) * SSD_STATE)
        b_g = bm_ref[:, gs].astype(BF16)
        c_g = cm_ref[:, gs].astype(BF16)
        cb = _nt_dot(c_g, b_g)
        for j in range(heads_per_group // 2):
            pair = (g * heads_per_group) // 2 + j
            ps = slice(pair * HEAD_WIDTH, (pair + 1) * HEAD_WIDTH)
            x_pair = xs_ref[:, ps]
            ws, acols, tots = [], [], []
            for hh in (2 * pair, 2 * pair + 1):
                acol = a_cum[:, hh:hh + 1]
                seg = acol - a_cum_t[hh:hh + 1, :]
                decay = jnp.exp(jnp.where(causal, seg, NEG_INF))
                ws.append(cb * decay * dt_t[hh:hh + 1, :])
                acols.append(acol)
                tots.append(total[:, hh:hh + 1])
            w2 = jnp.concatenate(ws, axis=1).astype(BF16)
            x2 = jnp.concatenate([jnp.where(lane_lo, x_pair, 0.0), jnp.where(lane_lo, 0.0, x_pair)],
                                 axis=0).astype(BF16)
            y_diag = jnp.dot(w2, x2, preferred_element_type=F32)
            h_in = state[ps, :]
            y_off = _nt_dot(c_g, h_in.astype(BF16)) * jnp.where(lane_lo, jnp.exp(acols[0]), jnp.exp(acols[1]))
            y_ref[:, ps] = y_diag + y_off
            to_end = [jnp.exp(tots[i] - acols[i]) * dt[:, 2 * pair + i: 2 * pair + i + 1] for i in range(2)]
            xw = x_pair * jnp.where(lane_lo, to_end[0], to_end[1])
            st = _tn_dot(xw.astype(BF16), b_g)
            dec = jnp.where(row_lo, jnp.exp(tots[0]), jnp.exp(tots[1]))
            state[ps, :] = h_in * dec + st

    if want_state:
        @pl.when(c == pl.num_programs(2) - 1)
        def _():
            hl_ref[...] = state[...]


def _ssd_scan(xbc, dt_raw, dt_bias, a_log, h0, row0_dt, batch, seq, n_heads, want_state):
    L = SSD_CHUNK
    nc = seq // L
    width = n_heads * SSD_HEAD_DIM
    n_groups = SSD_GROUPS
    bc = n_groups * SSD_STATE
    assert width % bc == 0 and row0_dt % L == 0
    r0 = row0_dt // L

    def rb(b, d, c):
        return b * nc + c + d * (nc - 1 - 2 * c)

    in_specs = [
        pl.BlockSpec((L, width), lambda b, d, c: (rb(b, d, c), 0)),
        pl.BlockSpec((L, bc), lambda b, d, c: (rb(b, d, c), width // bc)),
        pl.BlockSpec((L, bc), lambda b, d, c: (rb(b, d, c), width // bc + 1)),
        pl.BlockSpec((L, LANES), lambda b, d, c: (r0 + rb(b, d, c), d)),
        pl.BlockSpec((None, 1, LANES), lambda b, d, c: (d, 0, 0)),
        pl.BlockSpec((None, 1, LANES), lambda b, d, c: (d, 0, 0)),
    ]
    args = [xbc, xbc, xbc, dt_raw, dt_bias, a_log]
    has_h0 = h0 is not None
    if has_h0:
        in_specs.append(pl.BlockSpec((None, None, width, SSD_STATE), lambda b, d, c: (b, d, 0, 0)))
        args.append(h0)
    out_specs = [pl.BlockSpec((None, L, width), lambda b, d, c: (d, rb(b, d, c), 0))]
    out_shape = [jax.ShapeDtypeStruct((2, batch * seq, width), F32)]
    if want_state:
        out_specs.append(pl.BlockSpec((None, None, width, SSD_STATE), lambda b, d, c: (b, d, 0, 0)))
        out_shape.append(jax.ShapeDtypeStruct((batch, 2, width, SSD_STATE), F32))
    return pl.pallas_call(
        functools.partial(_ssd_kernel, n_heads=n_heads, heads_per_group=n_heads // n_groups,
                          has_h0=has_h0, want_state=want_state),
        grid=(batch, 2, nc),
        in_specs=in_specs,
        out_specs=out_specs,
        out_shape=out_shape,
        scratch_shapes=[pltpu.VMEM((width, SSD_STATE), F32)],
        compiler_params=_cparams(("arbitrary", "arbitrary", "arbitrary")),
        name="ssd_scan",
    )(*args)


def _ssd_gate_kernel(y_ref, xs_ref, z_ref, d_ref, nw_ref, o_ref):
    y = y_ref[0] + y_ref[1] + d_ref[...] * xs_ref[...]
    y = y * _silu(z_ref[...])
    o_ref[...] = _rms(y, nw_ref[...]).astype(o_ref.dtype)


def _ssd_gate(y2, xbc, proj, z_col0, row0, d_full, norm_w, width):
    n_rows = y2.shape[1]
    gw = width // SSD_GROUPS
    tm = _pick(n_rows, (512, 256, 128))
    assert z_col0 % gw == 0 and row0 % tm == 0
    zc, r0 = z_col0 // gw, row0 // tm
    return pl.pallas_call(
        _ssd_gate_kernel,
        grid=(n_rows // tm, SSD_GROUPS),
        in_specs=[
            pl.BlockSpec((2, tm, gw), lambda i, g: (0, i, g)),
            pl.BlockSpec((tm, gw), lambda i, g: (i, g)),
            pl.BlockSpec((tm, gw), lambda i, g: (r0 + i, zc + g)),
            pl.BlockSpec((1, gw), lambda i, g: (0, g)),
            pl.BlockSpec((1, gw), lambda i, g: (0, g)),
        ],
        out_specs=pl.BlockSpec((tm, gw), lambda i, g: (i, g)),
        out_shape=jax.ShapeDtypeStruct((n_rows, width), BF16),
        compiler_params=_cparams(("arbitrary", "arbitrary")),
        name="ssd_gate",
    )(y2, xbc, proj, d_full, norm_w)


def _outproj_kernel(d_ref, g_ref, s_ref, wd_ref, wg_ref, ws_ref, o_ref):
    acc = jnp.dot(d_ref[...], wd_ref[...], preferred_element_type=F32)
    acc += jnp.dot(g_ref[...], wg_ref[...], preferred_element_type=F32)
    acc += jnp.dot(s_ref[...], ws_ref[...], preferred_element_type=F32)
    o_ref[...] = acc


def _outproj(d_out, g_out, s_out, w_out):
    t = d_out.shape[0]
    dw, gw, sw = d_out.shape[1], g_out.shape[1], s_out.shape[1]
    d = w_out.shape[1]
    assert dw == gw and sw % dw == 0 and (dw + gw) % sw == 0
    tm = _pick(t, (512, 256, 128))
    tn = _pick(d, (1024, 512, 256, 128))
    return pl.pallas_call(
        _outproj_kernel,
        grid=(t // tm, d // tn),
        in_specs=[
            pl.BlockSpec((tm, dw), lambda i, j: (i, 0)),
            pl.BlockSpec((tm, gw), lambda i, j: (i, 0)),
            pl.BlockSpec((tm, sw), lambda i, j: (i, 0)),
            pl.BlockSpec((dw, tn), lambda i, j: (0, j)),
            pl.BlockSpec((gw, tn), lambda i, j: (1, j)),
            pl.BlockSpec((sw, tn), lambda i, j: ((dw + gw) // sw, j)),
        ],
        out_specs=pl.BlockSpec((tm, tn), lambda i, j: (i, j)),
        out_shape=jax.ShapeDtypeStruct((t, d), F32),
        compiler_params=_cparams(("arbitrary", "arbitrary")),
        name="out_proj",
    )(d_out, g_out, s_out, w_out, w_out, w_out)


def _first_index(hit, idx, sentinel):
    return jnp.min(jnp.where(hit, idx, sentinel), axis=0, keepdims=True)


def _router_kernel(u_ref, x_ref, mod_ref, g_ref, rw_ref, rb_ref,
                   x1_ref, h2_ref, idx_ref, wts_ref, rank_ref, cnt_ref, carry):
    per_group = N_EXPERTS // N_EXPERT_GROUPS

    @pl.when(pl.program_id(0) == 0)
    def _():
        carry[...] = jnp.zeros_like(carry)

    x1 = x_ref[...] + mod_ref[2:3, :] * _rms(u_ref[...], g_ref[1:2, :])
    x1_ref[...] = x1
    h2 = _rms(x1, g_ref[2:3, :]) * (1.0 + mod_ref[4:5, :]) + mod_ref[3:4, :]
    slab, pitch = _slab_pitch(h2.shape[1])
    _slab_store(h2_ref, _pack_rows(h2), slab, pitch)
    logits =_nt_dot(rw_ref[...], h2.astype(BF16))
    scores = 1.0 / (1.0 + jnp.exp(-logits))
    sel = scores + rb_ref[...]
    tm = sel.shape[1]
    sub = lax.broadcasted_iota(I32, (per_group, tm), 0)

    sel_g = [sel[g * per_group:(g + 1) * per_group, :] for g in range(N_EXPERT_GROUPS)]
    sc_g = [scores[g * per_group:(g + 1) * per_group, :] for g in range(N_EXPERT_GROUPS)]
    gscore = jnp.zeros((N_EXPERT_GROUPS, tm), F32)
    gsub = lax.broadcasted_iota(I32, (N_EXPERT_GROUPS, tm), 0)
    for g in range(N_EXPERT_GROUPS):
        v = sel_g[g]
        m1 = jnp.max(v, axis=0, keepdims=True)
        i1 = _first_index(v == m1, sub, per_group)
        m2 = jnp.max(jnp.where(sub == i1, NEG_INF, v), axis=0, keepdims=True)
        gscore = jnp.where(gsub == g, m1 + m2, gscore)
    gmask = jnp.zeros((N_EXPERT_GROUPS, tm), F32)
    gv = gscore
    for _ in range(TOPK_GROUPS):
        m = jnp.max(gv, axis=0, keepdims=True)
        hit = gsub == _first_index(gv == m, gsub, N_EXPERT_GROUPS)
        gmask = jnp.where(hit, 1.0, gmask)
        gv = jnp.where(hit, NEG_INF, gv)
    vals = [jnp.where(gmask[g:g + 1, :] > 0.0, sel_g[g], NEG_INF) for g in range(N_EXPERT_GROUPS)]
    eid = [sub + g * per_group for g in range(N_EXPERT_GROUPS)]

    picked = [jnp.zeros((per_group, tm), F32) for _ in range(N_EXPERT_GROUPS)]
    w_rows, idx_rows = [], []
    for _ in range(TOP_K):
        m = functools.reduce(jnp.maximum, [jnp.max(v, axis=0, keepdims=True) for v in vals])
        first = functools.reduce(jnp.minimum,
                                 [_first_index(v == m, e, N_EXPERTS) for v, e in zip(vals, eid)])
        w = jnp.zeros((1, tm), F32)
        for g in range(N_EXPERT_GROUPS):
            hit = eid[g] == first
            w = w + jnp.sum(jnp.where(hit, sc_g[g], 0.0), axis=0, keepdims=True)
            vals[g] = jnp.where(hit, NEG_INF, vals[g])
            picked[g] = jnp.where(hit, 1.0, picked[g])
        w_rows.append(w)
        idx_rows.append(first)
    wsum = functools.reduce(lambda a, b: a + b, w_rows)

    onehot = jnp.concatenate(picked, axis=0)
    ti = lax.broadcasted_iota(I32, (tm, tm), 0)
    tj = lax.broadcasted_iota(I32, (tm, tm), 1)
    upper = (ti < tj).astype(F32).astype(BF16)
    rank = jnp.dot(onehot.astype(BF16), upper, preferred_element_type=F32) + carry[:, 0:1]
    new_carry = carry[...] + jnp.sum(onehot, axis=1, keepdims=True)
    carry[...] = new_carry
    cnt_ref[...] = new_carry
    for k in range(TOP_K):
        idx_ref[k:k + 1, :] = idx_rows[k]
        wts_ref[k:k + 1, :] = w_rows[k] / wsum * ROUTED_SCALE
        r = jnp.zeros((1, tm), F32)
        for g in range(N_EXPERT_GROUPS):
            r = r + jnp.sum(jnp.where(eid[g] == idx_rows[k], rank[g * per_group:(g + 1) * per_group, :], 0.0),
                            axis=0, keepdims=True)
        rank_ref[k:k + 1, :] = r.astype(I32)


def _router(u, x, mod_l, g4, rw_t, rbias, toks):
    t, d = x.shape
    tm = min(256, _pick(toks.n_ctx, (256, 128)), _pick(toks.n_lat_seq, (256, 128)))
    _, pitch = _slab_pitch(d)
    row = lambda i: (i, 0)
    col = lambda i: (0, i)
    fixed = lambda i: (0, 0)
    return pl.pallas_call(
        _router_kernel,
        grid=(t // tm,),
        in_specs=[
            pl.BlockSpec((tm, d), row),
            pl.BlockSpec((tm, d), row),
            pl.BlockSpec((None, N_MOD, d), lambda i: (toks.mod_row(i, tm), 0, 0)),
            pl.BlockSpec((4, d), fixed),
            pl.BlockSpec((N_EXPERTS, d), fixed),
            pl.BlockSpec((N_EXPERTS, 1), fixed),
        ],
        out_specs=[
            pl.BlockSpec((tm, d), row),
            pl.BlockSpec((tm * pitch, LANES), row),
            pl.BlockSpec((TOP_K, tm), col),
            pl.BlockSpec((TOP_K, tm), col),
            pl.BlockSpec((TOP_K, tm), col),
            pl.BlockSpec((N_EXPERTS, LANES), fixed),
        ],
        out_shape=[
            jax.ShapeDtypeStruct((t, d), F32),
            jax.ShapeDtypeStruct((t * pitch, LANES), I32),
            jax.ShapeDtypeStruct((TOP_K, t), I32),
            jax.ShapeDtypeStruct((TOP_K, t), F32),
            jax.ShapeDtypeStruct((TOP_K, t), I32),
            jax.ShapeDtypeStruct((N_EXPERTS, LANES), F32),
        ],
        scratch_shapes=[pltpu.VMEM((N_EXPERTS, LANES), F32)],
        compiler_params=_cparams(("arbitrary",)),
        name="router",
    )(u, x, mod_l, g4, rw_t, rbias)


def _dispatch_kernel(pstart_ref, fill_ref, idx_ref, rank_ref, h_ref, hs_ref, zbuf, sem, *, pitch):
    tchunk = idx_ref.shape[1]
    blk = zbuf.shape[0]

    def row_copy(t, d):
        src = h_ref.at[pl.ds(pl.multiple_of(t * pitch, pitch), pitch)]
        dst = hs_ref.at[pl.ds(pl.multiple_of(d * pitch, pitch), pitch)]
        return pltpu.make_async_copy(src, dst, sem.at[0])

    @pl.when(pl.program_id(0) == 0)
    def _():
        zbuf[...] = jnp.zeros_like(zbuf)

        def fill_copy(e):
            start = pl.multiple_of(fill_ref[e] * pitch, blk)
            return pltpu.make_async_copy(zbuf, hs_ref.at[pl.ds(start, blk)], sem.at[1])

        def start(e, carry):
            @pl.when(fill_ref[e] >= 0)
            def _():
                fill_copy(e).start()
            return carry

        def wait(e, carry):
            @pl.when(fill_ref[e] >= 0)
            def _():
                fill_copy(e).wait()
            return carry

        lax.fori_loop(0, N_EXPERTS, start, 0)
        lax.fori_loop(0, N_EXPERTS, wait, 0)

    def issue(t, carry):
        for k in range(TOP_K):
            row_copy(t, pstart_ref[idx_ref[k, t]] + rank_ref[k, t]).start()
        return carry

    def drain(t, carry):
        for k in range(TOP_K):
            row_copy(0, 0).wait()
        return carry

    lax.fori_loop(0, tchunk, issue, 0)
    lax.fori_loop(0, tchunk, drain, 0)


def _dispatch(h2p, idx, rank, pad_start, fill_start, m_rows, pitch):
    t = h2p.shape[0] // pitch
    tchunk = _pick(t, (512, 256, 128))
    smem_blk = pl.BlockSpec((TOP_K, tchunk), lambda i, ps, fs: (0, i), memory_space=pltpu.SMEM)
    return pl.pallas_call(
        functools.partial(_dispatch_kernel, pitch=pitch),
        grid_spec=pltpu.PrefetchScalarGridSpec(
            num_scalar_prefetch=2,
            grid=(t // tchunk,),
            in_specs=[
                smem_blk,
                smem_blk,
                pl.BlockSpec((tchunk * pitch, LANES), lambda i, ps, fs: (i, 0)),
            ],
            out_specs=pl.BlockSpec(memory_space=pl.ANY),
            scratch_shapes=[pltpu.VMEM((EXPERT_ROWS * pitch, LANES), I32), pltpu.SemaphoreType.DMA((2,))],
        ),
        out_shape=jax.ShapeDtypeStruct((m_rows * pitch, LANES), I32),
        compiler_params=_cparams(("arbitrary",)),
        name="moe_dispatch",
    )(pad_start, fill_start, idx, rank, h2p)


def _swiglu_packed(h_ref, n, wg_ref, wu_ref, wd_ref):
    d = wg_ref.shape[0]
    slab, pitch = _slab_pitch(d)
    lo, hi = _unpack_rows(_slab_load(h_ref, n, slab, pitch))
    lo, hi = lo.astype(BF16), hi.astype(BF16)
    half = d // 2
    a = (jnp.dot(lo, wg_ref[:half, :], preferred_element_type=F32)
         + jnp.dot(hi, wg_ref[half:, :], preferred_element_type=F32))
    b = (jnp.dot(lo, wu_ref[:half, :], preferred_element_type=F32)
         + jnp.dot(hi, wu_ref[half:, :], preferred_element_type=F32))
    return jnp.dot((_silu(a) * b).astype(BF16), wd_ref[...], preferred_element_type=F32)


def _expert_kernel(blk_e_ref, n_used_ref, hs_ref, wg_ref, wu_ref, wd_ref, y_ref):
    @pl.when(pl.program_id(0) < n_used_ref[0])
    def _():
        y = _swiglu_packed(hs_ref, EXPERT_ROWS, wg_ref, wu_ref, wd_ref)
        slab, pitch = _slab_pitch(y.shape[1])
        _slab_store(y_ref, _pack_rows(y), slab, pitch)


def _experts(hs, blk_e, n_used, wg, wu, wd):
    d, ff = wg.shape[1], wg.shape[2]
    _, pitch = _slab_pitch(d)
    m_rows = hs.shape[0] // pitch
    n_blk = m_rows // EXPERT_ROWS

    def blk(i, blk_e, n_used):
        return jnp.minimum(i, n_used[0] - 1)

    return pl.pallas_call(
        _expert_kernel,
        grid_spec=pltpu.PrefetchScalarGridSpec(
            num_scalar_prefetch=2,
            grid=(n_blk,),
            in_specs=[
                pl.BlockSpec((EXPERT_ROWS * pitch, LANES), lambda i, be, nu: (blk(i, be, nu), 0)),
                pl.BlockSpec((None, d, ff), lambda i, be, nu: (be[blk(i, be, nu)], 0, 0)),
                pl.BlockSpec((None, d, ff), lambda i, be, nu: (be[blk(i, be, nu)], 0, 0)),
                pl.BlockSpec((None, ff, d), lambda i, be, nu: (be[blk(i, be, nu)], 0, 0)),
            ],
            out_specs=pl.BlockSpec((EXPERT_ROWS * pitch, LANES), lambda i, be, nu: (blk(i, be, nu), 0)),
        ),
        out_shape=jax.ShapeDtypeStruct((m_rows * pitch, LANES), I32),
        compiler_params=_cparams(("arbitrary",)),
        name="moe_experts",
    )(blk_e, n_used, hs, wg, wu, wd)


def _shared_kernel(h_ref, wg_ref, wu_ref, wd_ref, o_ref):
    o_ref[...] = _swiglu_packed(h_ref, o_ref.shape[0], wg_ref, wu_ref, wd_ref)


def _shared_expert(h2p, wg, wu, wd):
    d, ff = wg.shape
    _, pitch = _slab_pitch(d)
    t = h2p.shape[0] // pitch
    tm = _pick(t, (256, 128))
    fixed = lambda i: (0, 0)
    return pl.pallas_call(
        _shared_kernel,
        grid=(t // tm,),
        in_specs=[
            pl.BlockSpec((tm * pitch, LANES), lambda i: (i, 0)),
            pl.BlockSpec((d, ff), fixed),
            pl.BlockSpec((d, ff), fixed),
            pl.BlockSpec((ff, d), fixed),
        ],
        out_specs=pl.BlockSpec((tm, d), lambda i: (i, 0)),
        out_shape=jax.ShapeDtypeStruct((t, d), F32),
        compiler_params=_cparams(("arbitrary",)),
        name="shared_expert",
    )(h2p, wg, wu, wd)


def _combine_kernel(pstart_ref, idx_ref, rank_ref, y_ref, w_ref, sh_ref, x1_ref, mod_ref, g_ref, o_ref, ybuf, sem):
    tm, d_model = x1_ref.shape
    slab, pitch = _slab_pitch(d_model)
    half = d_model // 2

    def row_copy(k, t, d):
        src = y_ref.at[pl.ds(pl.multiple_of(d * pitch, pitch), pitch)]
        dst = ybuf.at[k, pl.ds(pl.multiple_of(t * pitch, pitch), pitch)]
        return pltpu.make_async_copy(src, dst, sem.at[0])

    def issue(t, carry):
        for k in range(TOP_K):
            row_copy(k, t, pstart_ref[idx_ref[k, t]] + rank_ref[k, t]).start()
        return carry

    def drain(t, carry):
        for k in range(TOP_K):
            row_copy(0, 0, 0).wait()
        return carry

    lax.fori_loop(0, tm, issue, 0)
    lax.fori_loop(0, tm, drain, 0)
    f_lo = sh_ref[:, :half]
    f_hi = sh_ref[:, half:]
    for k in range(TOP_K):
        lo, hi = _unpack_rows(_slab_load(ybuf, tm, slab, pitch, lead=(k,)))
        w = w_ref[:, k:k + 1]
        f_lo = f_lo + lo * w
        f_hi = f_hi + hi * w
    f = jnp.concatenate([f_lo, f_hi], axis=1)
    o_ref[...] = x1_ref[...] + mod_ref[5:6, :] * _rms(f, g_ref[3:4, :])


def _combine(y, idx, rank, pad_start, wts_t, sh, x1, mod_l, g4, toks):
    t, d = x1.shape
    _, pitch = _slab_pitch(d)
    tm = 128
    smem_blk = pl.BlockSpec((TOP_K, tm), lambda i, ps: (0, i), memory_space=pltpu.SMEM)
    return pl.pallas_call(
        _combine_kernel,
        grid_spec=pltpu.PrefetchScalarGridSpec(
            num_scalar_prefetch=1,
            grid=(t // tm,),
            in_specs=[
                smem_blk,
                smem_blk,
                pl.BlockSpec(memory_space=pl.ANY),
                pl.BlockSpec((tm, TOP_K), lambda i, ps: (i, 0)),
                pl.BlockSpec((tm, d), lambda i, ps: (i, 0)),
                pl.BlockSpec((tm, d), lambda i, ps: (i, 0)),
                pl.BlockSpec((None, N_MOD, d), lambda i, ps: (toks.mod_row(i, tm), 0, 0)),
                pl.BlockSpec((4, d), lambda i, ps: (0, 0)),
            ],
            out_specs=pl.BlockSpec((tm, d), lambda i, ps: (i, 0)),
            scratch_shapes=[pltpu.VMEM((TOP_K, tm * pitch, LANES), I32), pltpu.SemaphoreType.DMA((1,))],
        ),
        out_shape=jax.ShapeDtypeStruct((t, d), F32),
        compiler_params=_cparams(("arbitrary",)),
        name="moe_combine",
    )(pad_start, idx, rank, y, wts_t, sh, x1, mod_l, g4)


def _axial_tables(n_tokens, dim):
    rows = n_tokens // GRID_W
    row = jnp.repeat(jnp.arange(rows), GRID_W).astype(F32)
    col = (jnp.arange(rows * GRID_W) % GRID_W).astype(F32)
    n_freq = dim // 4
    inv = jnp.exp(-math.log(ROPE_THETA) * jnp.arange(n_freq, dtype=F32) / n_freq)
    ang = jnp.concatenate([row[:, None] * inv, col[:, None] * inv], axis=-1)
    cos, sin = jnp.cos(ang), jnp.sin(ang)
    reps = HEAD_WIDTH // dim
    cos_full = jnp.tile(jnp.concatenate([cos, cos], axis=-1), (1, reps))
    sin_full = jnp.tile(jnp.concatenate([-sin, sin], axis=-1), (1, reps))
    return cos_full, sin_full


def kernel(x_prompt, x_sample, cache_diff_k, cache_diff_v, cache_gqa_k, cache_gqa_v, state_ssd, c, c_ctx, w_ada, b_ada, norm_g, w_in, w_out, diff_lambda, diff_subnorm, gqa_qk_norm, ssd_conv_w, ssd_conv_b, ssd_dt_bias, ssd_a_log, ssd_d, ssd_norm, router_w, router_bias, exp_w_gate, exp_w_up, exp_w_down, sh_w_gate, sh_w_up, sh_w_down):
    batch, seq, d = x_prompt.shape
    dec_batch, dec_seq, _ = x_sample.shape
    depth = w_ada.shape[0]
    past = cache_diff_k.shape[2]
    n_dh = cache_diff_k.shape[3]
    n_kv = cache_gqa_k.shape[3]
    n_gh = (w_out.shape[1] - n_dh * HEAD_WIDTH - ssd_norm.shape[1]) // HEAD_WIDTH
    ssd_w = ssd_norm.shape[1]
    n_sh = ssd_w // SSD_HEAD_DIM
    bc = SSD_GROUPS * SSD_STATE
    dw, gw, kw = n_dh * HEAD_WIDTH, n_gh * HEAD_WIDTH, n_kv * HEAD_WIDTH
    assert 1 + dec_batch <= 8 and n_sh <= LANES

    t_ctx, t_lat = batch * seq, dec_batch * dec_seq
    t_all = t_ctx + t_lat
    toks = _Tokens(t_ctx, dec_seq)
    x = jnp.concatenate([x_prompt.reshape(t_ctx, d), x_sample.reshape(t_lat, d)], axis=0)

    cond = jnp.zeros((8, d), F32).at[0].set(c_ctx).at[1:1 + dec_batch].set(c)
    mod = _ada(cond, w_ada, b_ada).reshape(depth, 8, N_MOD, d)

    z_col = 3 * dw + gw + 2 * kw
    xbc_col = z_col + ssd_w
    dt_col = xbc_col + ssd_w + 2 * bc
    rope_d = _axial_tables(dec_seq, HEAD_WIDTH // 2)
    rope_g = _axial_tables(dec_seq, HEAD_WIDTH)

    tk_all = t_all * TOP_K
    n_blk = tk_all // EXPERT_ROWS + N_EXPERTS
    m_rows = n_blk * EXPERT_ROWS

    new_dk, new_dv, new_gk, new_gv, new_st = [], [], [], [], []
    for li in range(depth):
        w_in_l = w_in[li]
        w_main = w_in_l[:, :dt_col].astype(BF16)
        w_dt = jnp.zeros((d, 2 * LANES), F32)
        w_dt = w_dt.at[:, :n_sh].set(w_in_l[:, dt_col:dt_col + n_sh])
        w_dt = w_dt.at[:, LANES:LANES + n_sh].set(w_in_l[:, dt_col + n_sh:]).astype(BF16)
        proj, dt_raw = _inproj(x, mod[li], norm_g[li, 0:1], w_main, w_dt, toks)

        pad_heads = lambda v: jnp.zeros((2, 1, LANES), F32).at[:, 0, :n_sh].set(v)
        dt_bias, a_log = pad_heads(ssd_dt_bias[li]), pad_heads(ssd_a_log[li])
        d_full = jnp.repeat(ssd_d[li], SSD_HEAD_DIM)[None, :]
        norm_w = ssd_norm[li][None, :]
        lam_init = 0.8 - 0.6 * math.exp(-0.3 * li)
        diff_args = (diff_lambda[li], diff_subnorm[li][None, :], lam_init)

        qd, kd, vd, qg, kg, vg, kg_f32 = _prep(proj, gqa_qk_norm[li], 0, t_ctx, seq, n_dh, n_gh, n_kv, None)
        as_seq = lambda a, b, n: a.reshape(b, n, a.shape[-1])
        d_ctx = _attention(qd, [(as_seq(kd, batch, seq), as_seq(vd, batch, seq), seq, False)], n_dh, 1, batch, seq,
                           diff=diff_args)
        g_ctx = _attention(qg, [(as_seq(kg, batch, seq), as_seq(vg, batch, seq), seq, False)], n_gh, n_gh // n_kv,
                           batch, seq)
        xbc_c = _conv_silu(proj, xbc_col, ssd_w + 2 * bc, ssd_conv_w[li], ssd_conv_b[li][None, :], 0, batch, seq)
        y_c, h_last = _ssd_scan(xbc_c, dt_raw, dt_bias, a_log, None, 0, batch, seq, n_sh, True)
        s_ctx = _ssd_gate(y_c, xbc_c, proj, z_col, 0, d_full, norm_w, ssd_w)
        new_dk.append(proj[:t_ctx, dw:2 * dw].reshape(batch, seq, n_dh, 2, HEAD_WIDTH // 2))
        new_dv.append(proj[:t_ctx, 2 * dw:3 * dw].reshape(batch, seq, n_dh, HEAD_WIDTH))
        new_gk.append(kg_f32.reshape(batch, seq, n_kv, HEAD_WIDTH))
        new_gv.append(proj[:t_ctx, 3 * dw + gw + kw: 3 * dw + gw + 2 * kw].reshape(batch, seq, n_kv, HEAD_WIDTH))
        new_st.append(h_last.reshape(batch, 2, n_sh, SSD_HEAD_DIM, SSD_STATE))

        qd, kd, vd, qg, kg, vg = _prep(proj, gqa_qk_norm[li], t_ctx, t_lat, dec_seq, n_dh, n_gh, n_kv,
                                       rope_d + rope_g)
        c_dk = cache_diff_k[:, li].reshape(dec_batch, past, dw)
        c_dv = cache_diff_v[:, li].reshape(dec_batch, past, dw)
        c_gk = cache_gqa_k[:, li].reshape(dec_batch, past, kw)
        c_gv = cache_gqa_v[:, li].reshape(dec_batch, past, kw)
        d_lat = _attention(qd, [(c_dk, c_dv, past, False), (as_seq(kd, dec_batch, dec_seq), vd, dec_seq, True)],
                           n_dh, 1, dec_batch, dec_seq, diff=diff_args)
        g_lat = _attention(qg, [(c_gk, c_gv, past, False), (as_seq(kg, dec_batch, dec_seq), vg, dec_seq, True)],
                           n_gh, n_gh // n_kv, dec_batch, dec_seq)
        xbc_l = _conv_silu(proj, xbc_col, ssd_w + 2 * bc, ssd_conv_w[li], ssd_conv_b[li][None, :], t_ctx,
                           dec_batch, dec_seq)
        h0 = state_ssd[:, li].reshape(dec_batch, 2, ssd_w, SSD_STATE)
        (y_l,) = _ssd_scan(xbc_l, dt_raw, dt_bias, a_log, h0, t_ctx, dec_batch, dec_seq, n_sh, False)
        s_lat = _ssd_gate(y_l, xbc_l, proj, z_col, t_ctx, d_full, norm_w, ssd_w)

        cat = lambda a, b: jnp.concatenate([a, b], axis=0)
        u = _outproj(cat(d_ctx, d_lat), cat(g_ctx, g_lat), cat(s_ctx, s_lat), w_out[li].astype(BF16))
        x1, h2p, idx, wts, rank, counts = _router(u, x, mod[li], norm_g[li], router_w[li].T.astype(BF16),
                                                 router_bias[li][:, None], toks)

        counts = counts[:, 0].astype(I32)
        padded = (counts + EXPERT_ROWS - 1) // EXPERT_ROWS * EXPERT_ROWS
        pad_end = jnp.cumsum(padded)
        pad_start = (pad_end - padded).astype(I32)
        fill_start = jnp.where(counts > 0, pad_end - EXPERT_ROWS, -1).astype(I32)
        n_used = (pad_end[-1:] // EXPERT_ROWS).astype(I32)
        blk_row0 = jnp.arange(n_blk, dtype=I32) * EXPERT_ROWS
        blk_e = jnp.minimum(jnp.sum((pad_end[None, :] <= blk_row0[:, None]).astype(I32), axis=1), N_EXPERTS - 1)

        hs = _dispatch(h2p, idx, rank, pad_start, fill_start, m_rows, _slab_pitch(d)[1])
        y = _experts(hs, blk_e, n_used, exp_w_gate[li].astype(BF16), exp_w_up[li].astype(BF16),
                     exp_w_down[li].astype(BF16))
        sh = _shared_expert(h2p, sh_w_gate[li].astype(BF16), sh_w_up[li].astype(BF16), sh_w_down[li].astype(BF16))
        x = _combine(y, idx, rank, pad_start, wts.T, sh, x1, mod[li], norm_g[li], toks)

    y_prompt = x[:t_ctx].reshape(batch, seq, d)
    y_sample = x[t_ctx:].reshape(dec_batch, dec_seq, d)
    stack = lambda parts: jnp.stack(parts, axis=1)
    return (y_prompt, y_sample, stack(new_dk), stack(new_dv), stack(new_gk), stack(new_gv), stack(new_st))
```

```python
import functools
import math

import jax
import jax.numpy as jnp
from jax import lax
from jax.experimental import pallas as pl
from jax.experimental.pallas import tpu as pltpu

F32 = jnp.float32
BF16 = jnp.bfloat16
I32 = jnp.int32

LANES = 128
HEAD_WIDTH = 128
GRID_W = 64
GQA_KV_HEADS = 2
SSD_HEAD_DIM = 64
SSD_GROUPS = 4
SSD_STATE = 128
SSD_CHUNK = 128
ROPE_THETA = 10000.0
RMS_EPS = 1e-6
N_MOD = 6
N_EXPERTS = 64
TOP_K = 8
N_EXPERT_GROUPS = 8
TOPK_GROUPS = 4
ROUTED_SCALE = 2.5
EXPERT_ROWS = 256
ATTN_TQ = 512
ATTN_TK = 2048
VMEM_LIMIT = 56 * 1024 * 1024
NEG_INF = float("-inf")


def _cparams(sem):
    return pltpu.CompilerParams(dimension_semantics=sem, vmem_limit_bytes=VMEM_LIMIT)


def _pick(n, cands):
    for c in cands:
        if n % c == 0:
            return c
    raise ValueError(f"no tile for {n} in {cands}")


def _silu(x):
    return x * (1.0 / (1.0 + jnp.exp(-x)))


def _rms(x, g):
    return x * lax.rsqrt(jnp.mean(x * x, axis=-1, keepdims=True) + RMS_EPS) * g


def _nt_dot(a, b):
    return lax.dot_general(a, b, (((1,), (1,)), ((), ())), preferred_element_type=F32)


def _tn_dot(a, b):
    return lax.dot_general(a, b, (((0,), (0,)), ((), ())), preferred_element_type=F32)


HI_MASK = -65536


def _slab_pitch(d):
    slab = d // (2 * LANES)
    assert slab % 8 == 0
    return slab, (slab if (slab // 8) % 2 == 1 else slab + 8)


def _pack_rows(v):
    half = v.shape[1] // 2
    lo = lax.bitcast_convert_type(v[:, :half].astype(BF16).astype(F32), I32)
    hi = lax.bitcast_convert_type(v[:, half:].astype(BF16).astype(F32), I32)
    return lax.shift_right_logical(lo, 16) | (hi & HI_MASK)


def _unpack_rows(w):
    lo = lax.bitcast_convert_type(lax.shift_left(w, 16), F32)
    hi = lax.bitcast_convert_type(w & HI_MASK, F32)
    return lo, hi


def _slab_load(ref, n, slab, pitch, lead=()):
    return jnp.concatenate([ref[lead + (pl.ds(a, n, stride=pitch), slice(None))] for a in range(slab)], axis=1)


def _slab_store(ref, words, slab, pitch):
    n = words.shape[0]
    for a in range(slab):
        ref[pl.ds(a, n, stride=pitch), :] = words[:, a * LANES:(a + 1) * LANES]
    for a in range(slab, pitch):
        ref[pl.ds(a, n, stride=pitch), :] = jnp.zeros((n, LANES), I32)


def _ada_kernel(cond_ref, w_ref, b_ref, o_ref):
    s = _silu(cond_ref[...]).astype(BF16)
    o_ref[0] = jnp.dot(s, w_ref[0].astype(BF16), preferred_element_type=F32) + b_ref[0]


def _ada(cond, w_ada, b_ada):
    depth, d, n = w_ada.shape
    tn = _pick(n, (512, 256, 128))
    return pl.pallas_call(
        _ada_kernel,
        grid=(depth, n // tn),
        in_specs=[
            pl.BlockSpec((8, d), lambda l, j: (0, 0)),
            pl.BlockSpec((1, d, tn), lambda l, j: (l, 0, j)),
            pl.BlockSpec((1, 1, tn), lambda l, j: (l, 0, j)),
        ],
        out_specs=pl.BlockSpec((1, 8, tn), lambda l, j: (l, 0, j)),
        out_shape=jax.ShapeDtypeStruct((depth, 8, n), F32),
        compiler_params=_cparams(("arbitrary", "arbitrary")),
        name="ada_mod",
    )(cond, w_ada, b_ada.reshape(depth, 1, n))


class _Tokens:
    def __init__(self, n_ctx, n_lat_seq):
        self.n_ctx = n_ctx
        self.n_lat_seq = n_lat_seq

    def mod_row(self, i, tm):
        assert self.n_ctx % tm == 0 and self.n_lat_seq % tm == 0
        ctx_tiles = self.n_ctx // tm
        per_b = self.n_lat_seq // tm
        return jnp.where(i < ctx_tiles, 0, 1 + (i - ctx_tiles) // per_b)


def _inproj_kernel(x_ref, mod_ref, g_ref, w_ref, wdt_ref, o_ref, odt_ref, h_scr):
    @pl.when(pl.program_id(1) == 0)
    def _():
        h = _rms(x_ref[...], g_ref[...]) * (1.0 + mod_ref[1:2, :]) + mod_ref[0:1, :]
        hb = h.astype(BF16)
        h_scr[...] = hb
        odt_ref[...] = jnp.dot(hb, wdt_ref[...], preferred_element_type=F32)

    o_ref[...] = jnp.dot(h_scr[...], w_ref[...], preferred_element_type=F32)


def _inproj(x, mod_l, g0, w_main, w_dt, toks):
    t, d = x.shape
    n = w_main.shape[1]
    tm = _pick(t, (512, 256, 128))
    tm = min(tm, _pick(toks.n_ctx, (512, 256, 128)), _pick(toks.n_lat_seq, (512, 256, 128)))
    tn = _pick(n, (512, 256, 128))
    ndt = w_dt.shape[1]
    return pl.pallas_call(
        _inproj_kernel,
        grid=(t // tm, n // tn),
        in_specs=[
            pl.BlockSpec((tm, d), lambda i, j: (i, 0)),
            pl.BlockSpec((None, N_MOD, d), lambda i, j: (toks.mod_row(i, tm), 0, 0)),
            pl.BlockSpec((1, d), lambda i, j: (0, 0)),
            pl.BlockSpec((d, tn), lambda i, j: (0, j)),
            pl.BlockSpec((d, ndt), lambda i, j: (0, 0)),
        ],
        out_specs=[
            pl.BlockSpec((tm, tn), lambda i, j: (i, j)),
            pl.BlockSpec((tm, ndt), lambda i, j: (i, 0)),
        ],
        out_shape=[jax.ShapeDtypeStruct((t, n), F32), jax.ShapeDtypeStruct((t, ndt), F32)],
        scratch_shapes=[pltpu.VMEM((tm, d), BF16)],
        compiler_params=_cparams(("arbitrary", "arbitrary")),
        name="in_proj",
    )(x, mod_l, g0, w_main, w_dt)


def _attn_tq(seq):
    return min(ATTN_TQ, seq)


def _attn_tk(seq):
    return min(ATTN_TK, seq)


def _rope128(x, cos, sin, half):
    if half == 64:
        partner = pltpu.roll(x, 64, 1)
    else:
        lane = lax.broadcasted_iota(I32, x.shape, 1)
        partner = jnp.where((lane & (2 * half - 1)) < half, pltpu.roll(x, LANES - half, 1), pltpu.roll(x, half, 1))
    return x * cos + partner * sin


def _prep_kernel(*refs, n_dh, n_gh, n_kv, rope):
    if rope:
        p_ref, qkn_ref, cd_ref, sd_ref, cg_ref, sg_ref, qd_ref, kd_ref, vd_ref, qg_ref, kg_ref, vg_ref = refs
    else:
        p_ref, qkn_ref, qd_ref, kd_ref, vd_ref, qg_ref, kg_ref, vg_ref, kgf_ref = refs
    dw = n_dh * HEAD_WIDTH
    gw = n_gh * HEAD_WIDTH
    kw = n_kv * HEAD_WIDTH
    tn = p_ref.shape[0]
    d_scale = (HEAD_WIDTH // 2) ** -0.5
    g_scale = HEAD_WIDTH ** -0.5

    def put_v(v_ref, h, v):
        sl = slice(h * HEAD_WIDTH, (h + 1) * HEAD_WIDTH)
        if rope:
            v_ref[sl, :] = v.T.astype(BF16)
        else:
            v_ref[:, sl] = v.astype(BF16)

    for h in range(n_dh):
        sl = slice(h * HEAD_WIDTH, (h + 1) * HEAD_WIDTH)
        q = p_ref[:, sl]
        k = p_ref[:, dw + h * HEAD_WIDTH: dw + (h + 1) * HEAD_WIDTH]
        if rope:
            q = _rope128(q, cd_ref[...], sd_ref[...], HEAD_WIDTH // 4)
            k = _rope128(k, cd_ref[...], sd_ref[...], HEAD_WIDTH // 4)
        qd_ref[sl, :] = (q * d_scale).T.astype(BF16)
        kd_ref[:, sl] = k.astype(BF16)
        put_v(vd_ref, h, p_ref[:, 2 * dw + h * HEAD_WIDTH: 2 * dw + (h + 1) * HEAD_WIDTH])
    off = 3 * dw
    for h in range(n_gh):
        sl = slice(h * HEAD_WIDTH, (h + 1) * HEAD_WIDTH)
        q = _rms(p_ref[:, off + h * HEAD_WIDTH: off + (h + 1) * HEAD_WIDTH], qkn_ref[0:1, :])
        if rope:
            q = _rope128(q, cg_ref[...], sg_ref[...], HEAD_WIDTH // 2)
        qg_ref[sl, :] = (q * g_scale).T.astype(BF16)
    off += gw
    for h in range(n_kv):
        sl = slice(h * HEAD_WIDTH, (h + 1) * HEAD_WIDTH)
        k = _rms(p_ref[:, off + h * HEAD_WIDTH: off + (h + 1) * HEAD_WIDTH], qkn_ref[1:2, :])
        if rope:
            k = _rope128(k, cg_ref[...], sg_ref[...], HEAD_WIDTH // 2)
        else:
            kgf_ref[:, sl] = k
        kg_ref[:, sl] = k.astype(BF16)
        put_v(vg_ref, h, p_ref[:, off + kw + h * HEAD_WIDTH: off + kw + (h + 1) * HEAD_WIDTH])


def _prep(proj, qk_norm, row0, n_rows, seq, n_dh, n_gh, n_kv, rope_tabs):
    dw, gw, kw = n_dh * HEAD_WIDTH, n_gh * HEAD_WIDTH, n_kv * HEAD_WIDTH
    width = 3 * dw + gw + 2 * kw
    tn = _attn_tq(seq)
    tk = _attn_tk(seq)
    assert row0 % tn == 0 and seq % tk == 0 and tk % tn == 0
    r0 = row0 // tn
    n_tiles = n_rows // tn
    rope = rope_tabs is not None
    in_specs = [
        pl.BlockSpec((tn, width), lambda i: (r0 + i, 0)),
        pl.BlockSpec((2, HEAD_WIDTH), lambda i: (0, 0)),
    ]
    args = [proj, qk_norm]
    if rope:
        per_seq = seq // tn
        for tab in rope_tabs:
            in_specs.append(pl.BlockSpec((tn, HEAD_WIDTH), lambda i: (i % per_seq, 0)))
            args.append(tab)
    tok = lambda w: (pl.BlockSpec((tn, w), lambda i: (i, 0)), jax.ShapeDtypeStruct((n_rows, w), BF16))
    q_t = lambda w: (pl.BlockSpec((None, w, tn), lambda i: (i, 0, 0)), jax.ShapeDtypeStruct((n_tiles, w, tn), BF16))
    tpc = tk // tn
    v_t = lambda w: (pl.BlockSpec((None, w, tn), lambda i: (i // tpc, 0, i % tpc)),
                     jax.ShapeDtypeStruct((n_rows // tk, w, tk), BF16))
    v_out = v_t if rope else tok
    outs = [q_t(dw), tok(dw), v_out(dw), q_t(gw), tok(kw), v_out(kw)]
    if not rope:
        outs.append((pl.BlockSpec((tn, kw), lambda i: (i, 0)), jax.ShapeDtypeStruct((n_rows, kw), F32)))
    return pl.pallas_call(
        functools.partial(_prep_kernel, n_dh=n_dh, n_gh=n_gh, n_kv=n_kv, rope=rope),
        grid=(n_tiles,),
        in_specs=in_specs,
        out_specs=[o[0] for o in outs],
        out_shape=[o[1] for o in outs],
        compiler_params=_cparams(("arbitrary",)),
        name="attn_prep_rope" if rope else "attn_prep",
    )(*args)


def _softmax_step(state, s, pv):
    m, l, acc = state
    m_new = jnp.maximum(m, jnp.max(s, axis=0, keepdims=True))
    alpha = jnp.exp(m - m_new)
    p = jnp.exp(s - m_new)
    l = alpha * l + jnp.sum(p, axis=0, keepdims=True)
    acc = alpha * acc + pv(p.astype(BF16))
    return m_new, l, acc


def _attn_kernel(*refs, segs, diff, lam_init, hps, kv_group):
    n_seg = len(segs)
    q_ref = refs[0]
    kv_refs = refs[1: 1 + 2 * n_seg]
    s_scr = p_scr = None
    if any(v_t for _, _, v_t in segs):
        refs, (s_scr, p_scr) = refs[:-2], refs[-2:]
    if diff:
        lam_ref, sub_ref, o_ref = refs[1 + 2 * n_seg:]
    else:
        lam_ref = sub_ref = None
        (o_ref,) = refs[1 + 2 * n_seg:]
    for hh in range(hps):
        kvs = slice((hh // kv_group) * HEAD_WIDTH, (hh // kv_group + 1) * HEAD_WIDTH) if hps > 1 else slice(None)
        hs = slice(hh * HEAD_WIDTH, (hh + 1) * HEAD_WIDTH)
        _attn_head(q_ref, kv_refs, lam_ref, sub_ref, o_ref, s_scr, p_scr, hs, kvs, segs, diff, lam_init)


def _attn_head(q_ref, kv_refs, lam_ref, sub_ref, o_ref, s_scr, p_scr, hs, kvs, segs, diff, lam_init):
    qt = q_ref[hs, :]
    tq = qt.shape[1]
    if diff:
        row = lax.broadcasted_iota(I32, qt.shape, 0)
        qs = [jnp.where(row < HEAD_WIDTH // 2, qt, jnp.zeros_like(qt)),
              jnp.where(row < HEAD_WIDTH // 2, jnp.zeros_like(qt), qt)]
    else:
        qs = [qt]
    init = (jnp.full((1, tq), NEG_INF, F32), jnp.zeros((1, tq), F32), jnp.zeros((HEAD_WIDTH, tq), F32))
    states = tuple(init for _ in qs)

    for si, (t_len, tk, v_t) in enumerate(segs):
        k_ref, v_ref = kv_refs[2 * si], kv_refs[2 * si + 1]

        def chunk(c, sts, k_ref=k_ref, v_ref=v_ref, tk=tk, v_t=v_t):
            start = pl.multiple_of(c * tk, tk)
            k = k_ref[pl.ds(start, tk), kvs].astype(BF16)
            if v_t:
                vt = v_ref[c, kvs, :]
                pv = lambda p: jnp.dot(vt, p, preferred_element_type=F32)
            else:
                v = v_ref[pl.ds(start, tk), kvs].astype(BF16)
                pv = lambda p: _tn_dot(v, p)
            return tuple(_softmax_step(st, jnp.dot(k, qm, preferred_element_type=F32), pv)
                         for st, qm in zip(sts, qs))

        n_chunks = t_len // tk
        if n_chunks == 1:
            states = chunk(0, states)
        elif not v_t:
            states = lax.fori_loop(0, n_chunks, chunk, states)
        else:
            for c in range(n_chunks):
                slot = c % 2
                k = k_ref[c * tk:(c + 1) * tk, kvs]
                for i, qm in enumerate(qs):
                    s_scr[slot, i] = jnp.dot(k, qm, preferred_element_type=F32)
                new_states = []
                for i, (m, l, acc) in enumerate(states):
                    m_new = jnp.maximum(m, jnp.max(s_scr[slot, i], axis=0, keepdims=True))
                    alpha = jnp.exp(m - m_new)
                    p = jnp.exp(s_scr[slot, i] - m_new)
                    l = alpha * l + jnp.sum(p, axis=0, keepdims=True)
                    p_scr[slot, i] = p.astype(BF16)
                    acc = alpha * acc + jnp.dot(v_ref[c, kvs, :], p_scr[slot, i], preferred_element_type=F32)
                    new_states.append((m_new, l, acc))
                states = tuple(new_states)

    outs = [acc / l for (_, l, acc) in states]
    if diff:
        lp = lam_ref[...]
        lam = (jnp.exp(jnp.sum(lp[0:1] * lp[1:2], axis=-1, keepdims=True))
               - jnp.exp(jnp.sum(lp[2:3] * lp[3:4], axis=-1, keepdims=True)) + lam_init)
        o = outs[0] - lam * outs[1]
        o = o * lax.rsqrt(jnp.mean(o * o, axis=0, keepdims=True) + RMS_EPS) * (1.0 - lam_init)
        o = o.T * sub_ref[...]
    else:
        o = outs[0].T
    o_ref[:, hs] = o.astype(o_ref.dtype)


def _attention(qt, kvs, n_heads, kv_group, batch, seq, diff=None, hps=1):
    tq = _attn_tq(seq)
    per_seq = seq // tq
    assert hps == 1 or (hps % kv_group == 0 and n_heads % hps == 0)
    qw = hps * HEAD_WIDTH
    kvw = HEAD_WIDTH if hps == 1 else (hps // kv_group) * HEAD_WIDTH
    kv_blk = (lambda h: h // kv_group) if hps == 1 else (lambda h: h)
    in_specs = [pl.BlockSpec((None, qw, tq), lambda b, h, i: (b * per_seq + i, h, 0))]
    args = [qt]
    segs = []
    tk_t = 0
    for k, v, t_len, v_t in kvs:
        in_specs.append(pl.BlockSpec((None, t_len, kvw), lambda b, h, i: (b, 0, kv_blk(h))))
        if v_t:
            tk = tk_t = _attn_tk(t_len)
            in_specs.append(pl.BlockSpec((t_len // tk, kvw, tk), lambda b, h, i: (b, kv_blk(h), 0)))
        else:
            tk = _pick(t_len, (256, 128))
            in_specs.append(pl.BlockSpec((None, t_len, kvw), lambda b, h, i: (b, 0, kv_blk(h))))
        args += [k, v]
        segs.append((t_len, tk, v_t))
    lam_init = 0.0
    if diff is not None:
        lam_params, subnorm, lam_init = diff
        in_specs.append(pl.BlockSpec(lam_params.shape, lambda b, h, i: (0, 0)))
        in_specs.append(pl.BlockSpec((1, HEAD_WIDTH), lambda b, h, i: (0, 0)))
        args += [lam_params, subnorm]
    scratch = []
    if tk_t:
        n_maps = 2 if diff is not None else 1
        scratch = [pltpu.VMEM((2, n_maps, tk_t, tq), F32), pltpu.VMEM((2, n_maps, tk_t, tq), BF16)]
    return pl.pallas_call(
        functools.partial(_attn_kernel, segs=tuple(segs), diff=diff is not None, lam_init=lam_init,
                          hps=hps, kv_group=kv_group),
        grid=(batch, n_heads // hps, per_seq),
        scratch_shapes=scratch,
        in_specs=in_specs,
        out_specs=pl.BlockSpec((tq, qw), lambda b, h, i: (b * per_seq + i, h)),
        out_shape=jax.ShapeDtypeStruct((batch * seq, n_heads * HEAD_WIDTH), BF16),
        compiler_params=_cparams(("arbitrary", "arbitrary", "arbitrary")),
        name="diff_attn" if diff is not None else "gqa_attn",
    )(*args)


def _conv_kernel(u_ref, w_ref, b_ref, o_ref):
    u = u_ref[...]
    n = u.shape[0]
    row = lax.broadcasted_iota(I32, u.shape, 0)
    prev = jnp.where(row == 0, 0.0, pltpu.roll(u, 1, 0))
    nxt = jnp.where(row == n - 1, 0.0, pltpu.roll(u, n - 1, 0))
    y = w_ref[0:1, :] * prev + w_ref[1:2, :] * u + w_ref[2:3, :] * nxt + b_ref[...]
    o_ref[...] = _silu(y)


def _conv_silu(proj, col0, width, conv_w, conv_b, row0, batch, seq):
    tc = 256
    assert col0 % tc == 0 and width % tc == 0 and row0 % seq == 0
    c0, r0 = col0 // tc, row0 // seq
    return pl.pallas_call(
        _conv_kernel,
        grid=(batch, width // tc),
        in_specs=[
            pl.BlockSpec((seq, tc), lambda b, j: (r0 + b, c0 + j)),
            pl.BlockSpec((conv_w.shape[0], tc), lambda b, j: (0, j)),
            pl.BlockSpec((1, tc), lambda b, j: (0, j)),
        ],
        out_specs=pl.BlockSpec((seq, tc), lambda b, j: (b, j)),
        out_shape=jax.ShapeDtypeStruct((batch * seq, width), F32),
        compiler_params=_cparams(("arbitrary", "arbitrary")),
        name="ssd_conv",
    )(proj, conv_w, conv_b)


def _split3(x):
    hi = x.astype(BF16)
    r = x - hi.astype(F32)
    mid = r.astype(BF16)
    lo = (r - mid.astype(F32)).astype(BF16)
    return hi, mid, lo


def _ssd_kernel(*refs, n_heads, heads_per_group, has_h0, want_state):
    it = iter(refs)
    xs_ref, bm_ref, cm_ref, dt_ref, dtb_ref, alog_ref = (next(it) for _ in range(6))
    h0_ref = next(it) if has_h0 else None
    y_ref = next(it)
    hl_ref = next(it) if want_state else None
    state = next(it)

    fwd = pl.program_id(1) == 0
    c = pl.program_id(2)
    L = SSD_CHUNK

    @pl.when(c == 0)
    def _():
        if has_h0:
            state[...] = h0_ref[...]
        else:
            state[...] = jnp.zeros_like(state)

    z = dt_ref[...] + dtb_ref[...]
    dt = jnp.maximum(z, 0.0) + jnp.log1p(jnp.exp(-jnp.abs(z)))
    dta = dt * (-jnp.exp(alog_ref[...]))
    qi = lax.broadcasted_iota(I32, (L, L), 0)
    ki = lax.broadcasted_iota(I32, (L, L), 1)
    ahead = jnp.where(fwd, qi - ki, ki - qi)
    causal = ahead >= 0
    causal_t = ahead <= 0
    tri = causal.astype(F32).astype(BF16)
    tri_t = causal_t.astype(F32).astype(BF16)
    a_cum = sum(jnp.dot(tri, p, preferred_element_type=F32) for p in _split3(dta))
    a_cum_t = sum(jnp.dot(p, tri_t, preferred_element_type=F32) for p in _split3(dta.T))
    dt_t = dt.T
    total = jnp.sum(dta, axis=0, keepdims=True)
    lane_lo = lax.broadcasted_iota(I32, (L, HEAD_WIDTH), 1) < SSD_HEAD_DIM
    row_lo = lax.broadcasted_iota(I32, (HEAD_WIDTH, SSD_STATE), 0) < SSD_HEAD_DIM

    def spread(x):
        rows = lax.broadcasted_iota(I32, (LANES, n_heads * SSD_HEAD_DIM), 0)
        cols = lax.broadcasted_iota(I32, (LANES, n_heads * SSD_HEAD_DIM), 1)
        pick = (lax.shift_right_logical(cols, SSD_HEAD_DIM.bit_length() - 1) == rows).astype(F32).astype(BF16)
        return sum(jnp.dot(p, pick, preferred_element_type=F32) for p in _split3(x)[:2])

    ea_x = spread(jnp.exp(a_cum))
    te_x = spread(jnp.exp(total - a_cum) * dt)

    n_groups = n_heads // heads_per_group
    for g in range(n_groups):
        gs = slice(g * SSD_STATE, (g + 1) * SSD_STATE)
        b_g = bm_ref[:, gs].astype(BF16)
        c_g = cm_ref[:, gs].astype(BF16)
        cb = _nt_dot(c_g, b_g)
        for j in range(heads_per_group // 2):
            pair = (g * heads_per_group) // 2 + j
            ps = slice(pair * HEAD_WIDTH, (pair + 1) * HEAD_WIDTH)
            x_pair = xs_ref[:, ps]
            ws, tots = [], []
            for hh in (2 * pair, 2 * pair + 1):
                seg = a_cum[:, hh:hh + 1] - a_cum_t[hh:hh + 1, :]
                decay = jnp.exp(jnp.where(causal, seg, NEG_INF))
                ws.append(cb * decay * dt_t[hh:hh + 1, :])
                tots.append(total[:, hh:hh + 1])
            w2 = jnp.concatenate(ws, axis=1).astype(BF16)
            x2 = jnp.concatenate([jnp.where(lane_lo, x_pair, 0.0), jnp.where(lane_lo, 0.0, x_pair)],
                                 axis=0).astype(BF16)
            y_diag = jnp.dot(w2, x2, preferred_element_type=F32)
            h_in = state[ps, :]
            y_off = _nt_dot(c_g, h_in.astype(BF16)) * ea_x[:, ps]
            y_ref[:, ps] = y_diag + y_off
            xw = x_pair * te_x[:, ps]
            st = _tn_dot(xw.astype(BF16), b_g)
            dec = jnp.where(row_lo, jnp.exp(tots[0]), jnp.exp(tots[1]))
            state[ps, :] = h_in * dec + st

    if want_state:
        @pl.when(c == pl.num_programs(2) - 1)
        def _():
            hl_ref[...] = state[...]


def _ssd_scan(xbc, dt_raw, dt_bias, a_log, h0, row0_dt, batch, seq, n_heads, want_state):
    L = SSD_CHUNK
    nc = seq // L
    width = n_heads * SSD_HEAD_DIM
    n_groups = SSD_GROUPS
    bc = n_groups * SSD_STATE
    assert width % bc == 0 and row0_dt % L == 0
    r0 = row0_dt // L

    def rb(b, d, c):
        return b * nc + c + d * (nc - 1 - 2 * c)

    in_specs = [
        pl.BlockSpec((L, width), lambda b, d, c: (rb(b, d, c), 0)),
        pl.BlockSpec((L, bc), lambda b, d, c: (rb(b, d, c), width // bc)),
        pl.BlockSpec((L, bc), lambda b, d, c: (rb(b, d, c), width // bc + 1)),
        pl.BlockSpec((L, LANES), lambda b, d, c: (r0 + rb(b, d, c), d)),
        pl.BlockSpec((None, 1, LANES), lambda b, d, c: (d, 0, 0)),
        pl.BlockSpec((None, 1, LANES), lambda b, d, c: (d, 0, 0)),
    ]
    args = [xbc, xbc, xbc, dt_raw, dt_bias, a_log]
    has_h0 = h0 is not None
    if has_h0:
        in_specs.append(pl.BlockSpec((None, None, width, SSD_STATE), lambda b, d, c: (b, d, 0, 0)))
        args.append(h0)
    out_specs = [pl.BlockSpec((None, L, width), lambda b, d, c: (d, rb(b, d, c), 0))]
    out_shape = [jax.ShapeDtypeStruct((2, batch * seq, width), F32)]
    if want_state:
        out_specs.append(pl.BlockSpec((None, None, width, SSD_STATE), lambda b, d, c: (b, d, 0, 0)))
        out_shape.append(jax.ShapeDtypeStruct((batch, 2, width, SSD_STATE), F32))
    return pl.pallas_call(
        functools.partial(_ssd_kernel, n_heads=n_heads, heads_per_group=n_heads // n_groups,
                          has_h0=has_h0, want_state=want_state),
        grid=(batch, 2, nc),
        in_specs=in_specs,
        out_specs=out_specs,
        out_shape=out_shape,
        scratch_shapes=[pltpu.VMEM((width, SSD_STATE), F32)],
        compiler_params=_cparams(("arbitrary", "arbitrary", "arbitrary")),
        name="ssd_scan",
    )(*args)


def _ssd_gate_kernel(y_ref, xs_ref, z_ref, d_ref, nw_ref, o_ref):
    y = y_ref[0] + y_ref[1] + d_ref[...] * xs_ref[...]
    y = y * _silu(z_ref[...])
    o_ref[...] = _rms(y, nw_ref[...]).astype(o_ref.dtype)


def _ssd_gate(y2, xbc, proj, z_col0, row0, d_full, norm_w, width):
    n_rows = y2.shape[1]
    gw = width // SSD_GROUPS
    tm = _pick(n_rows, (512, 256, 128))
    assert z_col0 % gw == 0 and row0 % tm == 0
    zc, r0 = z_col0 // gw, row0 // tm
    return pl.pallas_call(
        _ssd_gate_kernel,
        grid=(n_rows // tm, SSD_GROUPS),
        in_specs=[
            pl.BlockSpec((2, tm, gw), lambda i, g: (0, i, g)),
            pl.BlockSpec((tm, gw), lambda i, g: (i, g)),
            pl.BlockSpec((tm, gw), lambda i, g: (r0 + i, zc + g)),
            pl.BlockSpec((1, gw), lambda i, g: (0, g)),
            pl.BlockSpec((1, gw), lambda i, g: (0, g)),
        ],
        out_specs=pl.BlockSpec((tm, gw), lambda i, g: (i, g)),
        out_shape=jax.ShapeDtypeStruct((n_rows, width), BF16),
        compiler_params=_cparams(("arbitrary", "arbitrary")),
        name="ssd_gate",
    )(y2, xbc, proj, d_full, norm_w)


def _outproj_kernel(d_ref, g_ref, s_ref, wd_ref, wg_ref, ws_ref, o_ref):
    acc = jnp.dot(d_ref[...], wd_ref[...], preferred_element_type=F32)
    acc += jnp.dot(g_ref[...], wg_ref[...], preferred_element_type=F32)
    acc += jnp.dot(s_ref[...], ws_ref[...], preferred_element_type=F32)
    o_ref[...] = acc


def _outproj(d_out, g_out, s_out, w_out):
    t = d_out.shape[0]
    dw, gw, sw = d_out.shape[1], g_out.shape[1], s_out.shape[1]
    d = w_out.shape[1]
    assert dw == gw and sw % dw == 0 and (dw + gw) % sw == 0
    tm = _pick(t, (512, 256, 128))
    tn = _pick(d, (1024, 512, 256, 128))
    return pl.pallas_call(
        _outproj_kernel,
        grid=(t // tm, d // tn),
        in_specs=[
            pl.BlockSpec((tm, dw), lambda i, j: (i, 0)),
            pl.BlockSpec((tm, gw), lambda i, j: (i, 0)),
            pl.BlockSpec((tm, sw), lambda i, j: (i, 0)),
            pl.BlockSpec((dw, tn), lambda i, j: (0, j)),
            pl.BlockSpec((gw, tn), lambda i, j: (1, j)),
            pl.BlockSpec((sw, tn), lambda i, j: ((dw + gw) // sw, j)),
        ],
        out_specs=pl.BlockSpec((tm, tn), lambda i, j: (i, j)),
        out_shape=jax.ShapeDtypeStruct((t, d), F32),
        compiler_params=_cparams(("arbitrary", "arbitrary")),
        name="out_proj",
    )(d_out, g_out, s_out, w_out, w_out, w_out)


def _first_index(hit, idx, sentinel):
    return jnp.min(jnp.where(hit, idx, sentinel), axis=0, keepdims=True)


def _router_kernel(u_ref, x_ref, mod_ref, g_ref, rw_ref, rb_ref,
                   x1_ref, h2_ref, idx_ref, wts_ref, rank_ref, cnt_ref, carry):
    per_group = N_EXPERTS // N_EXPERT_GROUPS

    @pl.when(pl.program_id(0) == 0)
    def _():
        carry[...] = jnp.zeros_like(carry)

    x1 = x_ref[...] + mod_ref[2:3, :] * _rms(u_ref[...], g_ref[1:2, :])
    x1_ref[...] = x1
    h2 = _rms(x1, g_ref[2:3, :]) * (1.0 + mod_ref[4:5, :]) + mod_ref[3:4, :]
    slab, pitch = _slab_pitch(h2.shape[1])
    _slab_store(h2_ref, _pack_rows(h2), slab, pitch)
    logits =_nt_dot(rw_ref[...], h2.astype(BF16))
    scores = 1.0 / (1.0 + jnp.exp(-logits))
    sel = scores + rb_ref[...]
    tm = sel.shape[1]
    sub = lax.broadcasted_iota(I32, (per_group, tm), 0)

    sel_g = [sel[g * per_group:(g + 1) * per_group, :] for g in range(N_EXPERT_GROUPS)]
    sc_g = [scores[g * per_group:(g + 1) * per_group, :] for g in range(N_EXPERT_GROUPS)]
    gscore = jnp.zeros((N_EXPERT_GROUPS, tm), F32)
    gsub = lax.broadcasted_iota(I32, (N_EXPERT_GROUPS, tm), 0)
    for g in range(N_EXPERT_GROUPS):
        v = sel_g[g]
        m1 = jnp.max(v, axis=0, keepdims=True)
        i1 = _first_index(v == m1, sub, per_group)
        m2 = jnp.max(jnp.where(sub == i1, NEG_INF, v), axis=0, keepdims=True)
        gscore = jnp.where(gsub == g, m1 + m2, gscore)
    gmask = jnp.zeros((N_EXPERT_GROUPS, tm), F32)
    gv = gscore
    for _ in range(TOPK_GROUPS):
        m = jnp.max(gv, axis=0, keepdims=True)
        hit = gsub == _first_index(gv == m, gsub, N_EXPERT_GROUPS)
        gmask = jnp.where(hit, 1.0, gmask)
        gv = jnp.where(hit, NEG_INF, gv)
    vals = [jnp.where(gmask[g:g + 1, :] > 0.0, sel_g[g], NEG_INF) for g in range(N_EXPERT_GROUPS)]
    eid = [sub + g * per_group for g in range(N_EXPERT_GROUPS)]

    picked = [jnp.zeros((per_group, tm), F32) for _ in range(N_EXPERT_GROUPS)]
    w_rows, idx_rows = [], []
    for _ in range(TOP_K):
        m = functools.reduce(jnp.maximum, [jnp.max(v, axis=0, keepdims=True) for v in vals])
        first = functools.reduce(jnp.minimum,
                                 [_first_index(v == m, e, N_EXPERTS) for v, e in zip(vals, eid)])
        w = jnp.zeros((1, tm), F32)
        for g in range(N_EXPERT_GROUPS):
            hit = eid[g] == first
            w = w + jnp.sum(jnp.where(hit, sc_g[g], 0.0), axis=0, keepdims=True)
            vals[g] = jnp.where(hit, NEG_INF, vals[g])
            picked[g] = jnp.where(hit, 1.0, picked[g])
        w_rows.append(w)
        idx_rows.append(first)
    wsum = functools.reduce(lambda a, b: a + b, w_rows)

    onehot = jnp.concatenate(picked, axis=0)
    ti = lax.broadcasted_iota(I32, (tm, tm), 0)
    tj = lax.broadcasted_iota(I32, (tm, tm), 1)
    upper = (ti < tj).astype(F32).astype(BF16)
    rank = jnp.dot(onehot.astype(BF16), upper, preferred_element_type=F32) + carry[:, 0:1]
    new_carry = carry[...] + jnp.sum(onehot, axis=1, keepdims=True)
    carry[...] = new_carry
    cnt_ref[...] = new_carry
    for k in range(TOP_K):
        idx_ref[k:k + 1, :] = idx_rows[k]
        wts_ref[k:k + 1, :] = w_rows[k] / wsum * ROUTED_SCALE
        r = jnp.zeros((1, tm), F32)
        for g in range(N_EXPERT_GROUPS):
            r = r + jnp.sum(jnp.where(eid[g] == idx_rows[k], rank[g * per_group:(g + 1) * per_group, :], 0.0),
                            axis=0, keepdims=True)
        rank_ref[k:k + 1, :] = r.astype(I32)


def _router(u, x, mod_l, g4, rw_t, rbias, toks):
    t, d = x.shape
    tm = min(256, _pick(toks.n_ctx, (256, 128)), _pick(toks.n_lat_seq, (256, 128)))
    _, pitch = _slab_pitch(d)
    row = lambda i: (i, 0)
    col = lambda i: (0, i)
    fixed = lambda i: (0, 0)
    return pl.pallas_call(
        _router_kernel,
        grid=(t // tm,),
        in_specs=[
            pl.BlockSpec((tm, d), row),
            pl.BlockSpec((tm, d), row),
            pl.BlockSpec((None, N_MOD, d), lambda i: (toks.mod_row(i, tm), 0, 0)),
            pl.BlockSpec((4, d), fixed),
            pl.BlockSpec((N_EXPERTS, d), fixed),
            pl.BlockSpec((N_EXPERTS, 1), fixed),
        ],
        out_specs=[
            pl.BlockSpec((tm, d), row),
            pl.BlockSpec((tm * pitch, LANES), row),
            pl.BlockSpec((TOP_K, tm), col),
            pl.BlockSpec((TOP_K, tm), col),
            pl.BlockSpec((TOP_K, tm), col),
            pl.BlockSpec((N_EXPERTS, LANES), fixed),
        ],
        out_shape=[
            jax.ShapeDtypeStruct((t, d), F32),
            jax.ShapeDtypeStruct((t * pitch, LANES), I32),
            jax.ShapeDtypeStruct((TOP_K, t), I32),
            jax.ShapeDtypeStruct((TOP_K, t), F32),
            jax.ShapeDtypeStruct((TOP_K, t), I32),
            jax.ShapeDtypeStruct((N_EXPERTS, LANES), F32),
        ],
        scratch_shapes=[pltpu.VMEM((N_EXPERTS, LANES), F32)],
        compiler_params=_cparams(("arbitrary",)),
        name="router",
    )(u, x, mod_l, g4, rw_t, rbias)


def _dispatch_kernel(pstart_ref, fill_ref, idx_ref, rank_ref, h_ref, hs_ref, zbuf, sem, *, pitch):
    tchunk = idx_ref.shape[1]
    blk = zbuf.shape[0]

    def row_copy(t, d):
        src = h_ref.at[pl.ds(pl.multiple_of(t * pitch, pitch), pitch)]
        dst = hs_ref.at[pl.ds(pl.multiple_of(d * pitch, pitch), pitch)]
        return pltpu.make_async_copy(src, dst, sem.at[0])

    @pl.when(pl.program_id(0) == 0)
    def _():
        zbuf[...] = jnp.zeros_like(zbuf)

        def fill_copy(e):
            start = pl.multiple_of(fill_ref[e] * pitch, blk)
            return pltpu.make_async_copy(zbuf, hs_ref.at[pl.ds(start, blk)], sem.at[1])

        def start(e, carry):
            @pl.when(fill_ref[e] >= 0)
            def _():
                fill_copy(e).start()
            return carry

        def wait(e, carry):
            @pl.when(fill_ref[e] >= 0)
            def _():
                fill_copy(e).wait()
            return carry

        lax.fori_loop(0, N_EXPERTS, start, 0)
        lax.fori_loop(0, N_EXPERTS, wait, 0)

    def issue(t, carry):
        for k in range(TOP_K):
            row_copy(t, pstart_ref[idx_ref[k, t]] + rank_ref[k, t]).start()
        return carry

    def drain(t, carry):
        for k in range(TOP_K):
            row_copy(0, 0).wait()
        return carry

    lax.fori_loop(0, tchunk, issue, 0)
    lax.fori_loop(0, tchunk, drain, 0)


def _dispatch(h2p, idx, rank, pad_start, fill_start, m_rows, pitch):
    t = h2p.shape[0] // pitch
    tchunk = _pick(t, (512, 256, 128))
    smem_blk = pl.BlockSpec((TOP_K, tchunk), lambda i, ps, fs: (0, i), memory_space=pltpu.SMEM)
    return pl.pallas_call(
        functools.partial(_dispatch_kernel, pitch=pitch),
        grid_spec=pltpu.PrefetchScalarGridSpec(
            num_scalar_prefetch=2,
            grid=(t // tchunk,),
            in_specs=[
                smem_blk,
                smem_blk,
                pl.BlockSpec((tchunk * pitch, LANES), lambda i, ps, fs: (i, 0)),
            ],
            out_specs=pl.BlockSpec(memory_space=pl.ANY),
            scratch_shapes=[pltpu.VMEM((EXPERT_ROWS * pitch, LANES), I32), pltpu.SemaphoreType.DMA((2,))],
        ),
        out_shape=jax.ShapeDtypeStruct((m_rows * pitch, LANES), I32),
        compiler_params=_cparams(("arbitrary",)),
        name="moe_dispatch",
    )(pad_start, fill_start, idx, rank, h2p)


def _cast_kernel(w_ref, o_ref):
    o_ref[...] = w_ref[...].astype(BF16)


def _cast_layer_bf16(w, li):
    _, e, a, b = w.shape
    return pl.pallas_call(
        _cast_kernel,
        grid=(e,),
        in_specs=[pl.BlockSpec((None, None, a, b), lambda i: (li, i, 0, 0))],
        out_specs=pl.BlockSpec((None, a, b), lambda i: (i, 0, 0)),
        out_shape=jax.ShapeDtypeStruct((e, a, b), BF16),
        compiler_params=_cparams(("arbitrary",)),
        name="cast_bf16",
    )(w)


def _swiglu_packed(h_ref, n, wg_ref, wu_ref, wd_ref):
    d = wg_ref.shape[0]
    slab, pitch = _slab_pitch(d)
    lo, hi = _unpack_rows(_slab_load(h_ref, n, slab, pitch))
    lo, hi = lo.astype(BF16), hi.astype(BF16)
    half = d // 2
    a = (jnp.dot(lo, wg_ref[:half, :], preferred_element_type=F32)
         + jnp.dot(hi, wg_ref[half:, :], preferred_element_type=F32))
    b = (jnp.dot(lo, wu_ref[:half, :], preferred_element_type=F32)
         + jnp.dot(hi, wu_ref[half:, :], preferred_element_type=F32))
    return jnp.dot((_silu(a) * b).astype(BF16), wd_ref[...], preferred_element_type=F32)


def _expert_kernel(blk_e_ref, n_used_ref, hs_ref, wg_ref, wu_ref, wd_ref, y_ref):
    @pl.when(pl.program_id(0) < n_used_ref[0])
    def _():
        y = _swiglu_packed(hs_ref, EXPERT_ROWS, wg_ref, wu_ref, wd_ref)
        slab, pitch = _slab_pitch(y.shape[1])
        _slab_store(y_ref, _pack_rows(y), slab, pitch)


def _experts(hs, blk_e, n_used, wg, wu, wd):
    d, ff = wg.shape[1], wg.shape[2]
    _, pitch = _slab_pitch(d)
    m_rows = hs.shape[0] // pitch
    n_blk = m_rows // EXPERT_ROWS

    def blk(i, blk_e, n_used):
        return jnp.minimum(i, n_used[0] - 1)

    return pl.pallas_call(
        _expert_kernel,
        grid_spec=pltpu.PrefetchScalarGridSpec(
            num_scalar_prefetch=2,
            grid=(n_blk,),
            in_specs=[
                pl.BlockSpec((EXPERT_ROWS * pitch, LANES), lambda i, be, nu: (blk(i, be, nu), 0)),
                pl.BlockSpec((None, d, ff), lambda i, be, nu: (be[blk(i, be, nu)], 0, 0)),
                pl.BlockSpec((None, d, ff), lambda i, be, nu: (be[blk(i, be, nu)], 0, 0)),
                pl.BlockSpec((None, ff, d), lambda i, be, nu: (be[blk(i, be, nu)], 0, 0)),
            ],
            out_specs=pl.BlockSpec((EXPERT_ROWS * pitch, LANES), lambda i, be, nu: (blk(i, be, nu), 0)),
        ),
        out_shape=jax.ShapeDtypeStruct((m_rows * pitch, LANES), I32),
        compiler_params=_cparams(("arbitrary",)),
        name="moe_experts",
    )(blk_e, n_used, hs, wg, wu, wd)


def _shared_kernel(h_ref, wg_ref, wu_ref, wd_ref, o_ref):
    o_ref[...] = _swiglu_packed(h_ref, o_ref.shape[0], wg_ref, wu_ref, wd_ref)


def _shared_expert(h2p, wg, wu, wd):
    d, ff = wg.shape
    _, pitch = _slab_pitch(d)
    t = h2p.shape[0] // pitch
    tm = _pick(t, (256, 128))
    fixed = lambda i: (0, 0)
    return pl.pallas_call(
        _shared_kernel,
        grid=(t // tm,),
        in_specs=[
            pl.BlockSpec((tm * pitch, LANES), lambda i: (i, 0)),
            pl.BlockSpec((d, ff), fixed),
            pl.BlockSpec((d, ff), fixed),
            pl.BlockSpec((ff, d), fixed),
        ],
        out_specs=pl.BlockSpec((tm, d), lambda i: (i, 0)),
        out_shape=jax.ShapeDtypeStruct((t, d), F32),
        compiler_params=_cparams(("arbitrary",)),
        name="shared_expert",
    )(h2p, wg, wu, wd)


def _combine_kernel(pstart_ref, idx_ref, rank_ref, y_ref, w_ref, sh_ref, x1_ref, mod_ref, g_ref, o_ref, ybuf, sem):
    tm, d_model = x1_ref.shape
    slab, pitch = _slab_pitch(d_model)
    half = d_model // 2

    def row_copy(k, t, d):
        src = y_ref.at[pl.ds(pl.multiple_of(d * pitch, pitch), pitch)]
        dst = ybuf.at[k, pl.ds(pl.multiple_of(t * pitch, pitch), pitch)]
        return pltpu.make_async_copy(src, dst, sem.at[0])

    def issue(t, carry):
        for k in range(TOP_K):
            row_copy(k, t, pstart_ref[idx_ref[k, t]] + rank_ref[k, t]).start()
        return carry

    def drain(t, carry):
        for k in range(TOP_K):
            row_copy(0, 0, 0).wait()
        return carry

    lax.fori_loop(0, tm, issue, 0)
    lax.fori_loop(0, tm, drain, 0)
    f_lo = sh_ref[:, :half]
    f_hi = sh_ref[:, half:]
    for k in range(TOP_K):
        lo, hi = _unpack_rows(_slab_load(ybuf, tm, slab, pitch, lead=(k,)))
        w = w_ref[:, k:k + 1]
        f_lo = f_lo + lo * w
        f_hi = f_hi + hi * w
    f = jnp.concatenate([f_lo, f_hi], axis=1)
    o_ref[...] = x1_ref[...] + mod_ref[5:6, :] * _rms(f, g_ref[3:4, :])


def _combine(y, idx, rank, pad_start, wts_t, sh, x1, mod_l, g4, toks):
    t, d = x1.shape
    _, pitch = _slab_pitch(d)
    tm = 128
    smem_blk = pl.BlockSpec((TOP_K, tm), lambda i, ps: (0, i), memory_space=pltpu.SMEM)
    return pl.pallas_call(
        _combine_kernel,
        grid_spec=pltpu.PrefetchScalarGridSpec(
            num_scalar_prefetch=1,
            grid=(t // tm,),
            in_specs=[
                smem_blk,
                smem_blk,
                pl.BlockSpec(memory_space=pl.ANY),
                pl.BlockSpec((tm, TOP_K), lambda i, ps: (i, 0)),
                pl.BlockSpec((tm, d), lambda i, ps: (i, 0)),
                pl.BlockSpec((tm, d), lambda i, ps: (i, 0)),
                pl.BlockSpec((None, N_MOD, d), lambda i, ps: (toks.mod_row(i, tm), 0, 0)),
                pl.BlockSpec((4, d), lambda i, ps: (0, 0)),
            ],
            out_specs=pl.BlockSpec((tm, d), lambda i, ps: (i, 0)),
            scratch_shapes=[pltpu.VMEM((TOP_K, tm * pitch, LANES), I32), pltpu.SemaphoreType.DMA((1,))],
        ),
        out_shape=jax.ShapeDtypeStruct((t, d), F32),
        compiler_params=_cparams(("arbitrary",)),
        name="moe_combine",
    )(pad_start, idx, rank, y, wts_t, sh, x1, mod_l, g4)


def _axial_tables(n_tokens, dim):
    rows = n_tokens // GRID_W
    row = jnp.repeat(jnp.arange(rows), GRID_W).astype(F32)
    col = (jnp.arange(rows * GRID_W) % GRID_W).astype(F32)
    n_freq = dim // 4
    inv = jnp.exp(-math.log(ROPE_THETA) * jnp.arange(n_freq, dtype=F32) / n_freq)
    ang = jnp.concatenate([row[:, None] * inv, col[:, None] * inv], axis=-1)
    cos, sin = jnp.cos(ang), jnp.sin(ang)
    reps = HEAD_WIDTH // dim
    cos_full = jnp.tile(jnp.concatenate([cos, cos], axis=-1), (1, reps))
    sin_full = jnp.tile(jnp.concatenate([-sin, sin], axis=-1), (1, reps))
    return cos_full, sin_full


def kernel(x_prompt, x_sample, cache_diff_k, cache_diff_v, cache_gqa_k, cache_gqa_v, state_ssd, c, c_ctx, w_ada, b_ada, norm_g, w_in, w_out, diff_lambda, diff_subnorm, gqa_qk_norm, ssd_conv_w, ssd_conv_b, ssd_dt_bias, ssd_a_log, ssd_d, ssd_norm, router_w, router_bias, exp_w_gate, exp_w_up, exp_w_down, sh_w_gate, sh_w_up, sh_w_down):
    batch, seq, d = x_prompt.shape
    dec_batch, dec_seq, _ = x_sample.shape
    depth = w_ada.shape[0]
    past = cache_diff_k.shape[2]
    n_dh = cache_diff_k.shape[3]
    n_kv = cache_gqa_k.shape[3]
    n_gh = (w_out.shape[1] - n_dh * HEAD_WIDTH - ssd_norm.shape[1]) // HEAD_WIDTH
    ssd_w = ssd_norm.shape[1]
    n_sh = ssd_w // SSD_HEAD_DIM
    bc = SSD_GROUPS * SSD_STATE
    dw, gw, kw = n_dh * HEAD_WIDTH, n_gh * HEAD_WIDTH, n_kv * HEAD_WIDTH
    assert 1 + dec_batch <= 8 and n_sh <= LANES

    t_ctx, t_lat = batch * seq, dec_batch * dec_seq
    t_all = t_ctx + t_lat
    toks = _Tokens(t_ctx, dec_seq)
    x = jnp.concatenate([x_prompt.reshape(t_ctx, d), x_sample.reshape(t_lat, d)], axis=0)

    cond = jnp.zeros((8, d), F32).at[0].set(c_ctx).at[1:1 + dec_batch].set(c)
    mod = _ada(cond, w_ada, b_ada).reshape(depth, 8, N_MOD, d)

    z_col = 3 * dw + gw + 2 * kw
    xbc_col = z_col + ssd_w
    dt_col = xbc_col + ssd_w + 2 * bc
    rope_d = _axial_tables(dec_seq, HEAD_WIDTH // 2)
    rope_g = _axial_tables(dec_seq, HEAD_WIDTH)

    tk_all = t_all * TOP_K
    n_blk = tk_all // EXPERT_ROWS + N_EXPERTS
    m_rows = n_blk * EXPERT_ROWS

    new_dk, new_dv, new_gk, new_gv, new_st = [], [], [], [], []
    for li in range(depth):
        w_in_l = w_in[li]
        w_main = w_in_l[:, :dt_col].astype(BF16)
        w_dt = jnp.zeros((d, 2 * LANES), F32)
        w_dt = w_dt.at[:, :n_sh].set(w_in_l[:, dt_col:dt_col + n_sh])
        w_dt = w_dt.at[:, LANES:LANES + n_sh].set(w_in_l[:, dt_col + n_sh:]).astype(BF16)
        proj, dt_raw = _inproj(x, mod[li], norm_g[li, 0:1], w_main, w_dt, toks)

        pad_heads = lambda v: jnp.zeros((2, 1, LANES), F32).at[:, 0, :n_sh].set(v)
        dt_bias, a_log = pad_heads(ssd_dt_bias[li]), pad_heads(ssd_a_log[li])
        d_full = jnp.repeat(ssd_d[li], SSD_HEAD_DIM)[None, :]
        norm_w = ssd_norm[li][None, :]
        lam_init = 0.8 - 0.6 * math.exp(-0.3 * li)
        diff_args = (diff_lambda[li], diff_subnorm[li][None, :], lam_init)

        qd, kd, vd, qg, kg, vg, kg_f32 = _prep(proj, gqa_qk_norm[li], 0, t_ctx, seq, n_dh, n_gh, n_kv, None)
        as_seq = lambda a, b, n: a.reshape(b, n, a.shape[-1])
        d_ctx = _attention(qd, [(as_seq(kd, batch, seq), as_seq(vd, batch, seq), seq, False)], n_dh, 1, batch, seq,
                           diff=diff_args, hps=n_dh)
        g_ctx = _attention(qg, [(as_seq(kg, batch, seq), as_seq(vg, batch, seq), seq, False)], n_gh, n_gh // n_kv,
                           batch, seq, hps=n_gh)
        xbc_c = _conv_silu(proj, xbc_col, ssd_w + 2 * bc, ssd_conv_w[li], ssd_conv_b[li][None, :], 0, batch, seq)
        y_c, h_last = _ssd_scan(xbc_c, dt_raw, dt_bias, a_log, None, 0, batch, seq, n_sh, True)
        s_ctx = _ssd_gate(y_c, xbc_c, proj, z_col, 0, d_full, norm_w, ssd_w)
        new_dk.append(proj[:t_ctx, dw:2 * dw].reshape(batch, seq, n_dh, 2, HEAD_WIDTH // 2))
        new_dv.append(proj[:t_ctx, 2 * dw:3 * dw].reshape(batch, seq, n_dh, HEAD_WIDTH))
        new_gk.append(kg_f32.reshape(batch, seq, n_kv, HEAD_WIDTH))
        new_gv.append(proj[:t_ctx, 3 * dw + gw + kw: 3 * dw + gw + 2 * kw].reshape(batch, seq, n_kv, HEAD_WIDTH))
        new_st.append(h_last.reshape(batch, 2, n_sh, SSD_HEAD_DIM, SSD_STATE))

        qd, kd, vd, qg, kg, vg = _prep(proj, gqa_qk_norm[li], t_ctx, t_lat, dec_seq, n_dh, n_gh, n_kv,
                                       rope_d + rope_g)
        c_dk = cache_diff_k[:, li].reshape(dec_batch, past, dw)
        c_dv = cache_diff_v[:, li].reshape(dec_batch, past, dw)
        c_gk = cache_gqa_k[:, li].reshape(dec_batch, past, kw)
        c_gv = cache_gqa_v[:, li].reshape(dec_batch, past, kw)
        d_lat = _attention(qd, [(c_dk, c_dv, past, False), (as_seq(kd, dec_batch, dec_seq), vd, dec_seq, True)],
                           n_dh, 1, dec_batch, dec_seq, diff=diff_args)
        g_lat = _attention(qg, [(c_gk, c_gv, past, False), (as_seq(kg, dec_batch, dec_seq), vg, dec_seq, True)],
                           n_gh, n_gh // n_kv, dec_batch, dec_seq)
        xbc_l = _conv_silu(proj, xbc_col, ssd_w + 2 * bc, ssd_conv_w[li], ssd_conv_b[li][None, :], t_ctx,
                           dec_batch, dec_seq)
        h0 = state_ssd[:, li].reshape(dec_batch, 2, ssd_w, SSD_STATE)
        (y_l,) = _ssd_scan(xbc_l, dt_raw, dt_bias, a_log, h0, t_ctx, dec_batch, dec_seq, n_sh, False)
        s_lat = _ssd_gate(y_l, xbc_l, proj, z_col, t_ctx, d_full, norm_w, ssd_w)

        cat = lambda a, b: jnp.concatenate([a, b], axis=0)
        u = _outproj(cat(d_ctx, d_lat), cat(g_ctx, g_lat), cat(s_ctx, s_lat), w_out[li].astype(BF16))
        x1, h2p, idx, wts, rank, counts = _router(u, x, mod[li], norm_g[li], router_w[li].T.astype(BF16),
                                                 router_bias[li][:, None], toks)

        counts = counts[:, 0].astype(I32)
        padded = (counts + EXPERT_ROWS - 1) // EXPERT_ROWS * EXPERT_ROWS
        pad_end = jnp.cumsum(padded)
        pad_start = (pad_end - padded).astype(I32)
        fill_start = jnp.where(counts > 0, pad_end - EXPERT_ROWS, -1).astype(I32)
        n_used = (pad_end[-1:] // EXPERT_ROWS).astype(I32)
        blk_row0 = jnp.arange(n_blk, dtype=I32) * EXPERT_ROWS
        blk_e = jnp.minimum(jnp.sum((pad_end[None, :] <= blk_row0[:, None]).astype(I32), axis=1), N_EXPERTS - 1)

        hs = _dispatch(h2p, idx, rank, pad_start, fill_start, m_rows, _slab_pitch(d)[1])
        y = _experts(hs, blk_e, n_used, _cast_layer_bf16(exp_w_gate, li), _cast_layer_bf16(exp_w_up, li),
                     _cast_layer_bf16(exp_w_down, li))
        sh = _shared_expert(h2p, sh_w_gate[li].astype(BF16), sh_w_up[li].astype(BF16), sh_w_down[li].astype(BF16))
        x = _combine(y, idx, rank, pad_start, wts.T, sh, x1, mod[li], norm_g[li], toks)

    y_prompt = x[:t_ctx].reshape(batch, seq, d)
    y_sample = x[t_ctx:].reshape(dec_batch, dec_seq, d)
    stack = lambda parts: jnp.stack(parts, axis=1)
    return (y_prompt, y_sample, stack(new_dk), stack(new_dv), stack(new_gk), stack(new_gv), stack(new_st))
```

```python
import functools
import math

import jax
import jax.numpy as jnp
from jax import lax
from jax.experimental import pallas as pl
from jax.experimental.pallas import tpu as pltpu

F32 = jnp.float32
BF16 = jnp.bfloat16
I32 = jnp.int32

LANES = 128
HEAD_WIDTH = 128
GRID_W = 64
GQA_KV_HEADS = 2
SSD_HEAD_DIM = 64
SSD_GROUPS = 4
SSD_STATE = 128
SSD_CHUNK = 128
ROPE_THETA = 10000.0
RMS_EPS = 1e-6
N_MOD = 6
N_EXPERTS = 64
TOP_K = 8
N_EXPERT_GROUPS = 8
TOPK_GROUPS = 4
ROUTED_SCALE = 2.5
EXPERT_ROWS = 256
ATTN_TQ = 512
ATTN_TK = 2048
VMEM_LIMIT = 56 * 1024 * 1024
NEG_INF = float("-inf")
LOG2E = 1.4426950408889634


def _cparams(sem):
    return pltpu.CompilerParams(dimension_semantics=sem, vmem_limit_bytes=VMEM_LIMIT)


def _pick(n, cands):
    for c in cands:
        if n % c == 0:
            return c
    raise ValueError(f"no tile for {n} in {cands}")


def _silu(x):
    return x * (1.0 / (1.0 + jnp.exp(-x)))


def _rms(x, g):
    return x * lax.rsqrt(jnp.mean(x * x, axis=-1, keepdims=True) + RMS_EPS) * g


def _nt_dot(a, b):
    return lax.dot_general(a, b, (((1,), (1,)), ((), ())), preferred_element_type=F32)


def _tn_dot(a, b):
    return lax.dot_general(a, b, (((0,), (0,)), ((), ())), preferred_element_type=F32)


HI_MASK = -65536


def _slab_pitch(d):
    slab = d // (2 * LANES)
    assert slab % 8 == 0
    return slab, (slab if (slab // 8) % 2 == 1 else slab + 8)


def _pack_rows(v):
    half = v.shape[1] // 2
    lo = lax.bitcast_convert_type(v[:, :half].astype(BF16).astype(F32), I32)
    hi = lax.bitcast_convert_type(v[:, half:].astype(BF16).astype(F32), I32)
    return lax.shift_right_logical(lo, 16) | (hi & HI_MASK)


def _unpack_rows(w):
    lo = lax.bitcast_convert_type(lax.shift_left(w, 16), F32)
    hi = lax.bitcast_convert_type(w & HI_MASK, F32)
    return lo, hi


def _slab_load(ref, n, slab, pitch, lead=()):
    return jnp.concatenate([ref[lead + (pl.ds(a, n, stride=pitch), slice(None))] for a in range(slab)], axis=1)


def _slab_store(ref, words, slab, pitch):
    n = words.shape[0]
    for a in range(slab):
        ref[pl.ds(a, n, stride=pitch), :] = words[:, a * LANES:(a + 1) * LANES]
    for a in range(slab, pitch):
        ref[pl.ds(a, n, stride=pitch), :] = jnp.zeros((n, LANES), I32)


def _ada_kernel(cond_ref, w_ref, b_ref, o_ref):
    s = _silu(cond_ref[...]).astype(BF16)
    o_ref[0] = jnp.dot(s, w_ref[0].astype(BF16), preferred_element_type=F32) + b_ref[0]


def _ada(cond, w_ada, b_ada):
    depth, d, n = w_ada.shape
    tn = _pick(n, (512, 256, 128))
    return pl.pallas_call(
        _ada_kernel,
        grid=(depth, n // tn),
        in_specs=[
            pl.BlockSpec((8, d), lambda l, j: (0, 0)),
            pl.BlockSpec((1, d, tn), lambda l, j: (l, 0, j)),
            pl.BlockSpec((1, 1, tn), lambda l, j: (l, 0, j)),
        ],
        out_specs=pl.BlockSpec((1, 8, tn), lambda l, j: (l, 0, j)),
        out_shape=jax.ShapeDtypeStruct((depth, 8, n), F32),
        compiler_params=_cparams(("arbitrary", "arbitrary")),
        name="ada_mod",
    )(cond, w_ada, b_ada.reshape(depth, 1, n))


class _Tokens:
    def __init__(self, n_ctx, n_lat_seq):
        self.n_ctx = n_ctx
        self.n_lat_seq = n_lat_seq

    def mod_row(self, i, tm):
        assert self.n_ctx % tm == 0 and self.n_lat_seq % tm == 0
        ctx_tiles = self.n_ctx // tm
        per_b = self.n_lat_seq // tm
        return jnp.where(i < ctx_tiles, 0, 1 + (i - ctx_tiles) // per_b)


def _inproj_kernel(x_ref, mod_ref, g_ref, w_ref, wdt_ref, o_ref, odt_ref, h_scr):
    @pl.when(pl.program_id(1) == 0)
    def _():
        h = _rms(x_ref[...], g_ref[...]) * (1.0 + mod_ref[1:2, :]) + mod_ref[0:1, :]
        hb = h.astype(BF16)
        h_scr[...] = hb
        odt_ref[...] = jnp.dot(hb, wdt_ref[...], preferred_element_type=F32)

    o_ref[...] = jnp.dot(h_scr[...], w_ref[...], preferred_element_type=F32)


def _inproj(x, mod_l, g0, w_main, w_dt, toks):
    t, d = x.shape
    n = w_main.shape[1]
    tm = _pick(t, (512, 256, 128))
    tm = min(tm, _pick(toks.n_ctx, (512, 256, 128)), _pick(toks.n_lat_seq, (512, 256, 128)))
    tn = _pick(n, (512, 256, 128))
    ndt = w_dt.shape[1]
    return pl.pallas_call(
        _inproj_kernel,
        grid=(t // tm, n // tn),
        in_specs=[
            pl.BlockSpec((tm, d), lambda i, j: (i, 0)),
            pl.BlockSpec((None, N_MOD, d), lambda i, j: (toks.mod_row(i, tm), 0, 0)),
            pl.BlockSpec((1, d), lambda i, j: (0, 0)),
            pl.BlockSpec((d, tn), lambda i, j: (0, j)),
            pl.BlockSpec((d, ndt), lambda i, j: (0, 0)),
        ],
        out_specs=[
            pl.BlockSpec((tm, tn), lambda i, j: (i, j)),
            pl.BlockSpec((tm, ndt), lambda i, j: (i, 0)),
        ],
        out_shape=[jax.ShapeDtypeStruct((t, n), F32), jax.ShapeDtypeStruct((t, ndt), F32)],
        scratch_shapes=[pltpu.VMEM((tm, d), BF16)],
        compiler_params=_cparams(("arbitrary", "arbitrary")),
        name="in_proj",
    )(x, mod_l, g0, w_main, w_dt)


def _attn_tq(seq):
    return min(ATTN_TQ, seq)


def _attn_tk(seq):
    return min(ATTN_TK, seq)


def _rope128(x, cos, sin, half):
    if half == 64:
        partner = pltpu.roll(x, 64, 1)
    else:
        lane = lax.broadcasted_iota(I32, x.shape, 1)
        partner = jnp.where((lane & (2 * half - 1)) < half, pltpu.roll(x, LANES - half, 1), pltpu.roll(x, half, 1))
    return x * cos + partner * sin


def _prep_kernel(*refs, n_dh, n_gh, n_kv, rope):
    if rope:
        p_ref, qkn_ref, cd_ref, sd_ref, cg_ref, sg_ref, qd_ref, kd_ref, vd_ref, qg_ref, kg_ref, vg_ref = refs
    else:
        p_ref, qkn_ref, qd_ref, kd_ref, vd_ref, qg_ref, kg_ref, vg_ref, kgf_ref = refs
    dw = n_dh * HEAD_WIDTH
    gw = n_gh * HEAD_WIDTH
    kw = n_kv * HEAD_WIDTH
    tn = p_ref.shape[0]
    d_scale = (HEAD_WIDTH // 2) ** -0.5 * LOG2E
    g_scale = HEAD_WIDTH ** -0.5 * LOG2E

    def put_v(v_ref, h, v):
        sl = slice(h * HEAD_WIDTH, (h + 1) * HEAD_WIDTH)
        if rope:
            v_ref[sl, :] = v.T.astype(BF16)
        else:
            v_ref[:, sl] = v.astype(BF16)

    for h in range(n_dh):
        sl = slice(h * HEAD_WIDTH, (h + 1) * HEAD_WIDTH)
        q = p_ref[:, sl]
        k = p_ref[:, dw + h * HEAD_WIDTH: dw + (h + 1) * HEAD_WIDTH]
        if rope:
            q = _rope128(q, cd_ref[...], sd_ref[...], HEAD_WIDTH // 4)
            k = _rope128(k, cd_ref[...], sd_ref[...], HEAD_WIDTH // 4)
        qd_ref[sl, :] = (q * d_scale).T.astype(BF16)
        kd_ref[:, sl] = k.astype(BF16)
        put_v(vd_ref, h, p_ref[:, 2 * dw + h * HEAD_WIDTH: 2 * dw + (h + 1) * HEAD_WIDTH])
    off = 3 * dw
    for h in range(n_gh):
        sl = slice(h * HEAD_WIDTH, (h + 1) * HEAD_WIDTH)
        q = _rms(p_ref[:, off + h * HEAD_WIDTH: off + (h + 1) * HEAD_WIDTH], qkn_ref[0:1, :])
        if rope:
            q = _rope128(q, cg_ref[...], sg_ref[...], HEAD_WIDTH // 2)
        qg_ref[sl, :] = (q * g_scale).T.astype(BF16)
    off += gw
    for h in range(n_kv):
        sl = slice(h * HEAD_WIDTH, (h + 1) * HEAD_WIDTH)
        k = _rms(p_ref[:, off + h * HEAD_WIDTH: off + (h + 1) * HEAD_WIDTH], qkn_ref[1:2, :])
        if rope:
            k = _rope128(k, cg_ref[...], sg_ref[...], HEAD_WIDTH // 2)
        else:
            kgf_ref[:, sl] = k
        kg_ref[:, sl] = k.astype(BF16)
        put_v(vg_ref, h, p_ref[:, off + kw + h * HEAD_WIDTH: off + kw + (h + 1) * HEAD_WIDTH])


def _prep(proj, qk_norm, row0, n_rows, seq, n_dh, n_gh, n_kv, rope_tabs):
    dw, gw, kw = n_dh * HEAD_WIDTH, n_gh * HEAD_WIDTH, n_kv * HEAD_WIDTH
    width = 3 * dw + gw + 2 * kw
    tn = _attn_tq(seq)
    tk = _attn_tk(seq)
    assert row0 % tn == 0 and seq % tk == 0 and tk % tn == 0
    r0 = row0 // tn
    n_tiles = n_rows // tn
    rope = rope_tabs is not None
    in_specs = [
        pl.BlockSpec((tn, width), lambda i: (r0 + i, 0)),
        pl.BlockSpec((2, HEAD_WIDTH), lambda i: (0, 0)),
    ]
    args = [proj, qk_norm]
    if rope:
        per_seq = seq // tn
        for tab in rope_tabs:
            in_specs.append(pl.BlockSpec((tn, HEAD_WIDTH), lambda i: (i % per_seq, 0)))
            args.append(tab)
    tok = lambda w: (pl.BlockSpec((tn, w), lambda i: (i, 0)), jax.ShapeDtypeStruct((n_rows, w), BF16))
    q_t = lambda w: (pl.BlockSpec((None, w, tn), lambda i: (i, 0, 0)), jax.ShapeDtypeStruct((n_tiles, w, tn), BF16))
    tpc = tk // tn
    v_t = lambda w: (pl.BlockSpec((None, w, tn), lambda i: (i // tpc, 0, i % tpc)),
                     jax.ShapeDtypeStruct((n_rows // tk, w, tk), BF16))
    v_out = v_t if rope else tok
    outs = [q_t(dw), tok(dw), v_out(dw), q_t(gw), tok(kw), v_out(kw)]
    if not rope:
        outs.append((pl.BlockSpec((tn, kw), lambda i: (i, 0)), jax.ShapeDtypeStruct((n_rows, kw), F32)))
    return pl.pallas_call(
        functools.partial(_prep_kernel, n_dh=n_dh, n_gh=n_gh, n_kv=n_kv, rope=rope),
        grid=(n_tiles,),
        in_specs=in_specs,
        out_specs=[o[0] for o in outs],
        out_shape=[o[1] for o in outs],
        compiler_params=_cparams(("arbitrary",)),
        name="attn_prep_rope" if rope else "attn_prep",
    )(*args)


def _softmax_step(state, s, pv):
    m, l, acc = state
    m_new = jnp.maximum(m, jnp.max(s, axis=0, keepdims=True))
    alpha = jnp.exp2(m - m_new)
    p = jnp.exp2(s - m_new)
    l = alpha * l + jnp.sum(p, axis=0, keepdims=True)
    acc = alpha * acc + pv(p.astype(BF16))
    return m_new, l, acc


def _attn_kernel(*refs, segs, diff, lam_init, hps, kv_group):
    n_seg = len(segs)
    q_ref = refs[0]
    kv_refs = refs[1: 1 + 2 * n_seg]
    s_scr = p_scr = None
    if any(v_t for _, _, v_t in segs):
        refs, (s_scr, p_scr) = refs[:-2], refs[-2:]
    if diff:
        lam_ref, sub_ref, o_ref = refs[1 + 2 * n_seg:]
    else:
        lam_ref = sub_ref = None
        (o_ref,) = refs[1 + 2 * n_seg:]
    for hh in range(hps):
        kvs = slice((hh // kv_group) * HEAD_WIDTH, (hh // kv_group + 1) * HEAD_WIDTH) if hps > 1 else slice(None)
        hs = slice(hh * HEAD_WIDTH, (hh + 1) * HEAD_WIDTH)
        _attn_head(q_ref, kv_refs, lam_ref, sub_ref, o_ref, s_scr, p_scr, hs, kvs, segs, diff, lam_init)


def _attn_head(q_ref, kv_refs, lam_ref, sub_ref, o_ref, s_scr, p_scr, hs, kvs, segs, diff, lam_init):
    qt = q_ref[hs, :]
    tq = qt.shape[1]
    if diff:
        row = lax.broadcasted_iota(I32, qt.shape, 0)
        qs = [jnp.where(row < HEAD_WIDTH // 2, qt, jnp.zeros_like(qt)),
              jnp.where(row < HEAD_WIDTH // 2, jnp.zeros_like(qt), qt)]
    else:
        qs = [qt]
    init = (jnp.full((1, tq), NEG_INF, F32), jnp.zeros((1, tq), F32), jnp.zeros((HEAD_WIDTH, tq), F32))
    states = tuple(init for _ in qs)

    for si, (t_len, tk, v_t) in enumerate(segs):
        k_ref, v_ref = kv_refs[2 * si], kv_refs[2 * si + 1]

        def chunk(c, sts, k_ref=k_ref, v_ref=v_ref, tk=tk, v_t=v_t):
            start = pl.multiple_of(c * tk, tk)
            k = k_ref[pl.ds(start, tk), kvs].astype(BF16)
            if v_t:
                vt = v_ref[c, kvs, :]
                pv = lambda p: jnp.dot(vt, p, preferred_element_type=F32)
            else:
                v = v_ref[pl.ds(start, tk), kvs].astype(BF16)
                pv = lambda p: _tn_dot(v, p)
            return tuple(_softmax_step(st, jnp.dot(k, qm, preferred_element_type=F32), pv)
                         for st, qm in zip(sts, qs))

        n_chunks = t_len // tk
        if n_chunks == 1:
            states = chunk(0, states)
        elif not v_t:
            states = lax.fori_loop(0, n_chunks, chunk, states)
        else:
            for c in range(n_chunks):
                slot = c % 2
                k = k_ref[c * tk:(c + 1) * tk, kvs]
                for i, qm in enumerate(qs):
                    s_scr[slot, i] = jnp.dot(k, qm, preferred_element_type=F32)
                new_states = []
                for i, (m, l, acc) in enumerate(states):
                    m_new = jnp.maximum(m, jnp.max(s_scr[slot, i], axis=0, keepdims=True))
                    alpha = jnp.exp2(m - m_new)
                    p = jnp.exp2(s_scr[slot, i] - m_new)
                    l = alpha * l + jnp.sum(p, axis=0, keepdims=True)
                    p_scr[slot, i] = p.astype(BF16)
                    acc = alpha * acc + jnp.dot(v_ref[c, kvs, :], p_scr[slot, i], preferred_element_type=F32)
                    new_states.append((m_new, l, acc))
                states = tuple(new_states)

    outs = [acc / l for (_, l, acc) in states]
    if diff:
        lp = lam_ref[...]
        lam = (jnp.exp(jnp.sum(lp[0:1] * lp[1:2], axis=-1, keepdims=True))
               - jnp.exp(jnp.sum(lp[2:3] * lp[3:4], axis=-1, keepdims=True)) + lam_init)
        o = outs[0] - lam * outs[1]
        o = o * lax.rsqrt(jnp.mean(o * o, axis=0, keepdims=True) + RMS_EPS) * (1.0 - lam_init)
        o = o.T * sub_ref[...]
    else:
        o = outs[0].T
    o_ref[:, hs] = o.astype(o_ref.dtype)


def _attention(qt, kvs, n_heads, kv_group, batch, seq, diff=None, hps=1):
    tq = _attn_tq(seq)
    per_seq = seq // tq
    assert hps == 1 or (hps % kv_group == 0 and n_heads % hps == 0)
    qw = hps * HEAD_WIDTH
    kvw = HEAD_WIDTH if hps == 1 else (hps // kv_group) * HEAD_WIDTH
    kv_blk = (lambda h: h // kv_group) if hps == 1 else (lambda h: h)
    in_specs = [pl.BlockSpec((None, qw, tq), lambda b, h, i: (b * per_seq + i, h, 0))]
    args = [qt]
    segs = []
    tk_t = 0
    for k, v, t_len, v_t in kvs:
        in_specs.append(pl.BlockSpec((None, t_len, kvw), lambda b, h, i: (b, 0, kv_blk(h))))
        if v_t:
            tk = tk_t = _attn_tk(t_len)
            in_specs.append(pl.BlockSpec((t_len // tk, kvw, tk), lambda b, h, i: (b, kv_blk(h), 0)))
        else:
            tk = _pick(t_len, (256, 128))
            in_specs.append(pl.BlockSpec((None, t_len, kvw), lambda b, h, i: (b, 0, kv_blk(h))))
        args += [k, v]
        segs.append((t_len, tk, v_t))
    lam_init = 0.0
    if diff is not None:
        lam_params, subnorm, lam_init = diff
        in_specs.append(pl.BlockSpec(lam_params.shape, lambda b, h, i: (0, 0)))
        in_specs.append(pl.BlockSpec((1, HEAD_WIDTH), lambda b, h, i: (0, 0)))
        args += [lam_params, subnorm]
    scratch = []
    if tk_t:
        n_maps = 2 if diff is not None else 1
        scratch = [pltpu.VMEM((2, n_maps, tk_t, tq), F32), pltpu.VMEM((2, n_maps, tk_t, tq), BF16)]
    return pl.pallas_call(
        functools.partial(_attn_kernel, segs=tuple(segs), diff=diff is not None, lam_init=lam_init,
                          hps=hps, kv_group=kv_group),
        grid=(batch, n_heads // hps, per_seq),
        scratch_shapes=scratch,
        in_specs=in_specs,
        out_specs=pl.BlockSpec((tq, qw), lambda b, h, i: (b * per_seq + i, h)),
        out_shape=jax.ShapeDtypeStruct((batch * seq, n_heads * HEAD_WIDTH), BF16),
        compiler_params=_cparams(("arbitrary", "arbitrary", "arbitrary")),
        name="diff_attn" if diff is not None else "gqa_attn",
    )(*args)


def _conv_kernel(u_ref, w_ref, b_ref, o_ref):
    u = u_ref[...]
    n = u.shape[0]
    row = lax.broadcasted_iota(I32, u.shape, 0)
    prev = jnp.where(row == 0, 0.0, pltpu.roll(u, 1, 0))
    nxt = jnp.where(row == n - 1, 0.0, pltpu.roll(u, n - 1, 0))
    y = w_ref[0:1, :] * prev + w_ref[1:2, :] * u + w_ref[2:3, :] * nxt + b_ref[...]
    o_ref[...] = _silu(y)


def _conv_silu(proj, col0, width, conv_w, conv_b, row0, batch, seq):
    tc = 256
    assert col0 % tc == 0 and width % tc == 0 and row0 % seq == 0
    c0, r0 = col0 // tc, row0 // seq
    return pl.pallas_call(
        _conv_kernel,
        grid=(batch, width // tc),
        in_specs=[
            pl.BlockSpec((seq, tc), lambda b, j: (r0 + b, c0 + j)),
            pl.BlockSpec((conv_w.shape[0], tc), lambda b, j: (0, j)),
            pl.BlockSpec((1, tc), lambda b, j: (0, j)),
        ],
        out_specs=pl.BlockSpec((seq, tc), lambda b, j: (b, j)),
        out_shape=jax.ShapeDtypeStruct((batch * seq, width), F32),
        compiler_params=_cparams(("arbitrary", "arbitrary")),
        name="ssd_conv",
    )(proj, conv_w, conv_b)


def _split3(x):
    hi = x.astype(BF16)
    r = x - hi.astype(F32)
    mid = r.astype(BF16)
    lo = (r - mid.astype(F32)).astype(BF16)
    return hi, mid, lo


def _ssd_kernel(*refs, n_heads, heads_per_group, has_h0, want_state):
    it = iter(refs)
    xs_ref, bm_ref, cm_ref, dt_ref, dtb_ref, alog_ref = (next(it) for _ in range(6))
    h0_ref = next(it) if has_h0 else None
    y_ref = next(it)
    hl_ref = next(it) if want_state else None
    state = next(it)

    fwd = pl.program_id(1) == 0
    c = pl.program_id(2)
    L = SSD_CHUNK

    @pl.when(c == 0)
    def _():
        if has_h0:
            state[...] = h0_ref[...]
        else:
            state[...] = jnp.zeros_like(state)

    z = dt_ref[...] + dtb_ref[...]
    dt = jnp.maximum(z, 0.0) + jnp.log1p(jnp.exp(-jnp.abs(z)))
    dta = dt * (-jnp.exp(alog_ref[...]))
    qi = lax.broadcasted_iota(I32, (L, L), 0)
    ki = lax.broadcasted_iota(I32, (L, L), 1)
    ahead = jnp.where(fwd, qi - ki, ki - qi)
    causal = ahead >= 0
    causal_t = ahead <= 0
    tri = causal.astype(F32).astype(BF16)
    tri_t = causal_t.astype(F32).astype(BF16)
    a_cum = sum(jnp.dot(tri, p, preferred_element_type=F32) for p in _split3(dta))
    a_cum_t = sum(jnp.dot(p, tri_t, preferred_element_type=F32) for p in _split3(dta.T))
    dt_t = dt.T
    total = jnp.sum(dta, axis=0, keepdims=True)
    lane_lo = lax.broadcasted_iota(I32, (L, HEAD_WIDTH), 1) < SSD_HEAD_DIM
    row_lo = lax.broadcasted_iota(I32, (HEAD_WIDTH, SSD_STATE), 0) < SSD_HEAD_DIM

    def spread(x):
        rows = lax.broadcasted_iota(I32, (LANES, n_heads * SSD_HEAD_DIM), 0)
        cols = lax.broadcasted_iota(I32, (LANES, n_heads * SSD_HEAD_DIM), 1)
        pick = (lax.shift_right_logical(cols, SSD_HEAD_DIM.bit_length() - 1) == rows).astype(F32).astype(BF16)
        return sum(jnp.dot(p, pick, preferred_element_type=F32) for p in _split3(x)[:2])

    ea_x = spread(jnp.exp(a_cum))
    te_x = spread(jnp.exp(total - a_cum) * dt)

    n_groups = n_heads // heads_per_group
    for g in range(n_groups):
        gs = slice(g * SSD_STATE, (g + 1) * SSD_STATE)
        b_g = bm_ref[:, gs].astype(BF16)
        c_g = cm_ref[:, gs].astype(BF16)
        cb = _nt_dot(c_g, b_g)
        for j in range(heads_per_group // 2):
            pair = (g * heads_per_group) // 2 + j
            ps = slice(pair * HEAD_WIDTH, (pair + 1) * HEAD_WIDTH)
            x_pair = xs_ref[:, ps]
            ws, tots = [], []
            for hh in (2 * pair, 2 * pair + 1):
                seg = a_cum[:, hh:hh + 1] - a_cum_t[hh:hh + 1, :]
                decay = jnp.exp(jnp.where(causal, seg, NEG_INF))
                ws.append(cb * decay * dt_t[hh:hh + 1, :])
                tots.append(total[:, hh:hh + 1])
            w2 = jnp.concatenate(ws, axis=1).astype(BF16)
            x2 = jnp.concatenate([jnp.where(lane_lo, x_pair, 0.0), jnp.where(lane_lo, 0.0, x_pair)],
                                 axis=0).astype(BF16)
            y_diag = jnp.dot(w2, x2, preferred_element_type=F32)
            h_in = state[ps, :]
            y_off = _nt_dot(c_g, h_in.astype(BF16)) * ea_x[:, ps]
            y_ref[:, ps] = y_diag + y_off
            xw = x_pair * te_x[:, ps]
            st = _tn_dot(xw.astype(BF16), b_g)
            dec = jnp.where(row_lo, jnp.exp(tots[0]), jnp.exp(tots[1]))
            state[ps, :] = h_in * dec + st

    if want_state:
        @pl.when(c == pl.num_programs(2) - 1)
        def _():
            hl_ref[...] = state[...]


def _ssd_scan(xbc, dt_raw, dt_bias, a_log, h0, row0_dt, batch, seq, n_heads, want_state):
    L = SSD_CHUNK
    nc = seq // L
    width = n_heads * SSD_HEAD_DIM
    n_groups = SSD_GROUPS
    bc = n_groups * SSD_STATE
    assert width % bc == 0 and row0_dt % L == 0
    r0 = row0_dt // L

    def rb(b, d, c):
        return b * nc + c + d * (nc - 1 - 2 * c)

    in_specs = [
        pl.BlockSpec((L, width), lambda b, d, c: (rb(b, d, c), 0)),
        pl.BlockSpec((L, bc), lambda b, d, c: (rb(b, d, c), width // bc)),
        pl.BlockSpec((L, bc), lambda b, d, c: (rb(b, d, c), width // bc + 1)),
        pl.BlockSpec((L, LANES), lambda b, d, c: (r0 + rb(b, d, c), d)),
        pl.BlockSpec((None, 1, LANES), lambda b, d, c: (d, 0, 0)),
        pl.BlockSpec((None, 1, LANES), lambda b, d, c: (d, 0, 0)),
    ]
    args = [xbc, xbc, xbc, dt_raw, dt_bias, a_log]
    has_h0 = h0 is not None
    if has_h0:
        in_specs.append(pl.BlockSpec((None, None, width, SSD_STATE), lambda b, d, c: (b, d, 0, 0)))
        args.append(h0)
    out_specs = [pl.BlockSpec((None, L, width), lambda b, d, c: (d, rb(b, d, c), 0))]
    out_shape = [jax.ShapeDtypeStruct((2, batch * seq, width), F32)]
    if want_state:
        out_specs.append(pl.BlockSpec((None, None, width, SSD_STATE), lambda b, d, c: (b, d, 0, 0)))
        out_shape.append(jax.ShapeDtypeStruct((batch, 2, width, SSD_STATE), F32))
    return pl.pallas_call(
        functools.partial(_ssd_kernel, n_heads=n_heads, heads_per_group=n_heads // n_groups,
                          has_h0=has_h0, want_state=want_state),
        grid=(batch, 2, nc),
        in_specs=in_specs,
        out_specs=out_specs,
        out_shape=out_shape,
        scratch_shapes=[pltpu.VMEM((width, SSD_STATE), F32)],
        compiler_params=_cparams(("arbitrary", "arbitrary", "arbitrary")),
        name="ssd_scan",
    )(*args)


def _ssd_gate_kernel(y_ref, xs_ref, z_ref, d_ref, nw_ref, o_ref):
    y = y_ref[0] + y_ref[1] + d_ref[...] * xs_ref[...]
    y = y * _silu(z_ref[...])
    o_ref[...] = _rms(y, nw_ref[...]).astype(o_ref.dtype)


def _ssd_gate(y2, xbc, proj, z_col0, row0, d_full, norm_w, width):
    n_rows = y2.shape[1]
    gw = width // SSD_GROUPS
    tm = _pick(n_rows, (512, 256, 128))
    assert z_col0 % gw == 0 and row0 % tm == 0
    zc, r0 = z_col0 // gw, row0 // tm
    return pl.pallas_call(
        _ssd_gate_kernel,
        grid=(n_rows // tm, SSD_GROUPS),
        in_specs=[
            pl.BlockSpec((2, tm, gw), lambda i, g: (0, i, g)),
            pl.BlockSpec((tm, gw), lambda i, g: (i, g)),
            pl.BlockSpec((tm, gw), lambda i, g: (r0 + i, zc + g)),
            pl.BlockSpec((1, gw), lambda i, g: (0, g)),
            pl.BlockSpec((1, gw), lambda i, g: (0, g)),
        ],
        out_specs=pl.BlockSpec((tm, gw), lambda i, g: (i, g)),
        out_shape=jax.ShapeDtypeStruct((n_rows, width), BF16),
        compiler_params=_cparams(("arbitrary", "arbitrary")),
        name="ssd_gate",
    )(y2, xbc, proj, d_full, norm_w)


def _outproj_kernel(d_ref, g_ref, s_ref, wd_ref, wg_ref, ws_ref, o_ref):
    acc = jnp.dot(d_ref[...], wd_ref[...], preferred_element_type=F32)
    acc += jnp.dot(g_ref[...], wg_ref[...], preferred_element_type=F32)
    acc += jnp.dot(s_ref[...], ws_ref[...], preferred_element_type=F32)
    o_ref[...] = acc


def _outproj(d_out, g_out, s_out, w_out):
    t = d_out.shape[0]
    dw, gw, sw = d_out.shape[1], g_out.shape[1], s_out.shape[1]
    d = w_out.shape[1]
    assert dw == gw and sw % dw == 0 and (dw + gw) % sw == 0
    tm = _pick(t, (512, 256, 128))
    tn = _pick(d, (1024, 512, 256, 128))
    return pl.pallas_call(
        _outproj_kernel,
        grid=(t // tm, d // tn),
        in_specs=[
            pl.BlockSpec((tm, dw), lambda i, j: (i, 0)),
            pl.BlockSpec((tm, gw), lambda i, j: (i, 0)),
            pl.BlockSpec((tm, sw), lambda i, j: (i, 0)),
            pl.BlockSpec((dw, tn), lambda i, j: (0, j)),
            pl.BlockSpec((gw, tn), lambda i, j: (1, j)),
            pl.BlockSpec((sw, tn), lambda i, j: ((dw + gw) // sw, j)),
        ],
        out_specs=pl.BlockSpec((tm, tn), lambda i, j: (i, j)),
        out_shape=jax.ShapeDtypeStruct((t, d), F32),
        compiler_params=_cparams(("arbitrary", "arbitrary")),
        name="out_proj",
    )(d_out, g_out, s_out, w_out, w_out, w_out)


def _first_index(hit, idx, sentinel):
    return jnp.min(jnp.where(hit, idx, sentinel), axis=0, keepdims=True)


def _router_kernel(u_ref, x_ref, mod_ref, g_ref, rw_ref, rb_ref,
                   x1_ref, h2_ref, idx_ref, wts_ref, rank_ref, cnt_ref, carry):
    per_group = N_EXPERTS // N_EXPERT_GROUPS

    @pl.when(pl.program_id(0) == 0)
    def _():
        carry[...] = jnp.zeros_like(carry)

    x1 = x_ref[...] + mod_ref[2:3, :] * _rms(u_ref[...], g_ref[1:2, :])
    x1_ref[...] = x1
    h2 = _rms(x1, g_ref[2:3, :]) * (1.0 + mod_ref[4:5, :]) + mod_ref[3:4, :]
    slab, pitch = _slab_pitch(h2.shape[1])
    _slab_store(h2_ref, _pack_rows(h2), slab, pitch)
    logits =_nt_dot(rw_ref[...], h2.astype(BF16))
    scores = 1.0 / (1.0 + jnp.exp(-logits))
    sel = scores + rb_ref[...]
    tm = sel.shape[1]
    sub = lax.broadcasted_iota(I32, (per_group, tm), 0)

    sel_g = [sel[g * per_group:(g + 1) * per_group, :] for g in range(N_EXPERT_GROUPS)]
    sc_g = [scores[g * per_group:(g + 1) * per_group, :] for g in range(N_EXPERT_GROUPS)]
    gscore = jnp.zeros((N_EXPERT_GROUPS, tm), F32)
    gsub = lax.broadcasted_iota(I32, (N_EXPERT_GROUPS, tm), 0)
    for g in range(N_EXPERT_GROUPS):
        v = sel_g[g]
        m1 = jnp.max(v, axis=0, keepdims=True)
        i1 = _first_index(v == m1, sub, per_group)
        m2 = jnp.max(jnp.where(sub == i1, NEG_INF, v), axis=0, keepdims=True)
        gscore = jnp.where(gsub == g, m1 + m2, gscore)
    gmask = jnp.zeros((N_EXPERT_GROUPS, tm), F32)
    gv = gscore
    for _ in range(TOPK_GROUPS):
        m = jnp.max(gv, axis=0, keepdims=True)
        hit = gsub == _first_index(gv == m, gsub, N_EXPERT_GROUPS)
        gmask = jnp.where(hit, 1.0, gmask)
        gv = jnp.where(hit, NEG_INF, gv)
    vals = [jnp.where(gmask[g:g + 1, :] > 0.0, sel_g[g], NEG_INF) for g in range(N_EXPERT_GROUPS)]
    eid = [sub + g * per_group for g in range(N_EXPERT_GROUPS)]

    picked = [jnp.zeros((per_group, tm), F32) for _ in range(N_EXPERT_GROUPS)]
    w_rows, idx_rows = [], []
    for _ in range(TOP_K):
        m = functools.reduce(jnp.maximum, [jnp.max(v, axis=0, keepdims=True) for v in vals])
        first = functools.reduce(jnp.minimum,
                                 [_first_index(v == m, e, N_EXPERTS) for v, e in zip(vals, eid)])
        w = jnp.zeros((1, tm), F32)
        for g in range(N_EXPERT_GROUPS):
            hit = eid[g] == first
            w = w + jnp.sum(jnp.where(hit, sc_g[g], 0.0), axis=0, keepdims=True)
            vals[g] = jnp.where(hit, NEG_INF, vals[g])
            picked[g] = jnp.where(hit, 1.0, picked[g])
        w_rows.append(w)
        idx_rows.append(first)
    wsum = functools.reduce(lambda a, b: a + b, w_rows)

    onehot = jnp.concatenate(picked, axis=0)
    ti = lax.broadcasted_iota(I32, (tm, tm), 0)
    tj = lax.broadcasted_iota(I32, (tm, tm), 1)
    upper = (ti < tj).astype(F32).astype(BF16)
    rank = jnp.dot(onehot.astype(BF16), upper, preferred_element_type=F32) + carry[:, 0:1]
    new_carry = carry[...] + jnp.sum(onehot, axis=1, keepdims=True)
    carry[...] = new_carry
    cnt_ref[...] = new_carry
    for k in range(TOP_K):
        idx_ref[k:k + 1, :] = idx_rows[k]
        wts_ref[k:k + 1, :] = w_rows[k] / wsum * ROUTED_SCALE
        r = jnp.zeros((1, tm), F32)
        for g in range(N_EXPERT_GROUPS):
            r = r + jnp.sum(jnp.where(eid[g] == idx_rows[k], rank[g * per_group:(g + 1) * per_group, :], 0.0),
                            axis=0, keepdims=True)
        rank_ref[k:k + 1, :] = r.astype(I32)


def _router(u, x, mod_l, g4, rw_t, rbias, toks):
    t, d = x.shape
    tm = min(256, _pick(toks.n_ctx, (256, 128)), _pick(toks.n_lat_seq, (256, 128)))
    _, pitch = _slab_pitch(d)
    row = lambda i: (i, 0)
    col = lambda i: (0, i)
    fixed = lambda i: (0, 0)
    return pl.pallas_call(
        _router_kernel,
        grid=(t // tm,),
        in_specs=[
            pl.BlockSpec((tm, d), row),
            pl.BlockSpec((tm, d), row),
            pl.BlockSpec((None, N_MOD, d), lambda i: (toks.mod_row(i, tm), 0, 0)),
            pl.BlockSpec((4, d), fixed),
            pl.BlockSpec((N_EXPERTS, d), fixed),
            pl.BlockSpec((N_EXPERTS, 1), fixed),
        ],
        out_specs=[
            pl.BlockSpec((tm, d), row),
            pl.BlockSpec((tm * pitch, LANES), row),
            pl.BlockSpec((TOP_K, tm), col),
            pl.BlockSpec((TOP_K, tm), col),
            pl.BlockSpec((TOP_K, tm), col),
            pl.BlockSpec((N_EXPERTS, LANES), fixed),
        ],
        out_shape=[
            jax.ShapeDtypeStruct((t, d), F32),
            jax.ShapeDtypeStruct((t * pitch, LANES), I32),
            jax.ShapeDtypeStruct((TOP_K, t), I32),
            jax.ShapeDtypeStruct((TOP_K, t), F32),
            jax.ShapeDtypeStruct((TOP_K, t), I32),
            jax.ShapeDtypeStruct((N_EXPERTS, LANES), F32),
        ],
        scratch_shapes=[pltpu.VMEM((N_EXPERTS, LANES), F32)],
        compiler_params=_cparams(("arbitrary",)),
        name="router",
    )(u, x, mod_l, g4, rw_t, rbias)


def _dispatch_kernel(pstart_ref, fill_ref, idx_ref, rank_ref, h_ref, hs_ref, zbuf, sem, *, pitch):
    tchunk = idx_ref.shape[1]
    blk = zbuf.shape[0]

    def row_copy(t, d):
        src = h_ref.at[pl.ds(pl.multiple_of(t * pitch, pitch), pitch)]
        dst = hs_ref.at[pl.ds(pl.multiple_of(d * pitch, pitch), pitch)]
        return pltpu.make_async_copy(src, dst, sem.at[0])

    @pl.when(pl.program_id(0) == 0)
    def _():
        zbuf[...] = jnp.zeros_like(zbuf)

        def fill_copy(e):
            start = pl.multiple_of(fill_ref[e] * pitch, blk)
            return pltpu.make_async_copy(zbuf, hs_ref.at[pl.ds(start, blk)], sem.at[1])

        def start(e, carry):
            @pl.when(fill_ref[e] >= 0)
            def _():
                fill_copy(e).start()
            return carry

        def wait(e, carry):
            @pl.when(fill_ref[e] >= 0)
            def _():
                fill_copy(e).wait()
            return carry

        lax.fori_loop(0, N_EXPERTS, start, 0)
        lax.fori_loop(0, N_EXPERTS, wait, 0)

    def issue(t, carry):
        for k in range(TOP_K):
            row_copy(t, pstart_ref[idx_ref[k, t]] + rank_ref[k, t]).start()
        return carry

    def drain(t, carry):
        for k in range(TOP_K):
            row_copy(0, 0).wait()
        return carry

    lax.fori_loop(0, tchunk, issue, 0)
    lax.fori_loop(0, tchunk, drain, 0)


def _dispatch(h2p, idx, rank, pad_start, fill_start, m_rows, pitch):
    t = h2p.shape[0] // pitch
    tchunk = _pick(t, (512, 256, 128))
    smem_blk = pl.BlockSpec((TOP_K, tchunk), lambda i, ps, fs: (0, i), memory_space=pltpu.SMEM)
    return pl.pallas_call(
        functools.partial(_dispatch_kernel, pitch=pitch),
        grid_spec=pltpu.PrefetchScalarGridSpec(
            num_scalar_prefetch=2,
            grid=(t // tchunk,),
            in_specs=[
                smem_blk,
                smem_blk,
                pl.BlockSpec((tchunk * pitch, LANES), lambda i, ps, fs: (i, 0)),
            ],
            out_specs=pl.BlockSpec(memory_space=pl.ANY),
            scratch_shapes=[pltpu.VMEM((EXPERT_ROWS * pitch, LANES), I32), pltpu.SemaphoreType.DMA((2,))],
        ),
        out_shape=jax.ShapeDtypeStruct((m_rows * pitch, LANES), I32),
        compiler_params=_cparams(("arbitrary",)),
        name="moe_dispatch",
    )(pad_start, fill_start, idx, rank, h2p)


def _cast_kernel(w_ref, o_ref):
    o_ref[...] = w_ref[...].astype(BF16)


def _cast_layer_bf16(w, li):
    _, e, a, b = w.shape
    return pl.pallas_call(
        _cast_kernel,
        grid=(e,),
        in_specs=[pl.BlockSpec((None, None, a, b), lambda i: (li, i, 0, 0))],
        out_specs=pl.BlockSpec((None, a, b), lambda i: (i, 0, 0)),
        out_shape=jax.ShapeDtypeStruct((e, a, b), BF16),
        compiler_params=_cparams(("arbitrary",)),
        name="cast_bf16",
    )(w)


def _swiglu_packed(h_ref, n, wg_ref, wu_ref, wd_ref):
    d = wg_ref.shape[0]
    slab, pitch = _slab_pitch(d)
    lo, hi = _unpack_rows(_slab_load(h_ref, n, slab, pitch))
    lo, hi = lo.astype(BF16), hi.astype(BF16)
    half = d // 2
    a = (jnp.dot(lo, wg_ref[:half, :], preferred_element_type=F32)
         + jnp.dot(hi, wg_ref[half:, :], preferred_element_type=F32))
    b = (jnp.dot(lo, wu_ref[:half, :], preferred_element_type=F32)
         + jnp.dot(hi, wu_ref[half:, :], preferred_element_type=F32))
    return jnp.dot((_silu(a) * b).astype(BF16), wd_ref[...], preferred_element_type=F32)


def _expert_kernel(blk_e_ref, n_used_ref, hs_ref, wg_ref, wu_ref, wd_ref, y_ref):
    @pl.when(pl.program_id(0) < n_used_ref[0])
    def _():
        y = _swiglu_packed(hs_ref, EXPERT_ROWS, wg_ref, wu_ref, wd_ref)
        slab, pitch = _slab_pitch(y.shape[1])
        _slab_store(y_ref, _pack_rows(y), slab, pitch)


def _experts(hs, blk_e, n_used, wg, wu, wd):
    d, ff = wg.shape[1], wg.shape[2]
    _, pitch = _slab_pitch(d)
    m_rows = hs.shape[0] // pitch
    n_blk = m_rows // EXPERT_ROWS

    def blk(i, blk_e, n_used):
        return jnp.minimum(i, n_used[0] - 1)

    return pl.pallas_call(
        _expert_kernel,
        grid_spec=pltpu.PrefetchScalarGridSpec(
            num_scalar_prefetch=2,
            grid=(n_blk,),
            in_specs=[
                pl.BlockSpec((EXPERT_ROWS * pitch, LANES), lambda i, be, nu: (blk(i, be, nu), 0)),
                pl.BlockSpec((None, d, ff), lambda i, be, nu: (be[blk(i, be, nu)], 0, 0)),
                pl.BlockSpec((None, d, ff), lambda i, be, nu: (be[blk(i, be, nu)], 0, 0)),
                pl.BlockSpec((None, ff, d), lambda i, be, nu: (be[blk(i, be, nu)], 0, 0)),
            ],
            out_specs=pl.BlockSpec((EXPERT_ROWS * pitch, LANES), lambda i, be, nu: (blk(i, be, nu), 0)),
        ),
        out_shape=jax.ShapeDtypeStruct((m_rows * pitch, LANES), I32),
        compiler_params=_cparams(("arbitrary",)),
        name="moe_experts",
    )(blk_e, n_used, hs, wg, wu, wd)


def _shared_kernel(h_ref, wg_ref, wu_ref, wd_ref, o_ref):
    o_ref[...] = _swiglu_packed(h_ref, o_ref.shape[0], wg_ref, wu_ref, wd_ref)


def _shared_expert(h2p, wg, wu, wd):
    d, ff = wg.shape
    _, pitch = _slab_pitch(d)
    t = h2p.shape[0] // pitch
    tm = _pick(t, (256, 128))
    fixed = lambda i: (0, 0)
    return pl.pallas_call(
        _shared_kernel,
        grid=(t // tm,),
        in_specs=[
            pl.BlockSpec((tm * pitch, LANES), lambda i: (i, 0)),
            pl.BlockSpec((d, ff), fixed),
            pl.BlockSpec((d, ff), fixed),
            pl.BlockSpec((ff, d), fixed),
        ],
        out_specs=pl.BlockSpec((tm, d), lambda i: (i, 0)),
        out_shape=jax.ShapeDtypeStruct((t, d), F32),
        compiler_params=_cparams(("arbitrary",)),
        name="shared_expert",
    )(h2p, wg, wu, wd)


def _combine_kernel(pstart_ref, idx_ref, rank_ref, y_ref, w_ref, sh_ref, x1_ref, mod_ref, g_ref, o_ref, ybuf, sem):
    tm, d_model = x1_ref.shape
    slab, pitch = _slab_pitch(d_model)
    half = d_model // 2

    def row_copy(k, t, d):
        src = y_ref.at[pl.ds(pl.multiple_of(d * pitch, pitch), pitch)]
        dst = ybuf.at[k, pl.ds(pl.multiple_of(t * pitch, pitch), pitch)]
        return pltpu.make_async_copy(src, dst, sem.at[0])

    def issue(t, carry):
        for k in range(TOP_K):
            row_copy(k, t, pstart_ref[idx_ref[k, t]] + rank_ref[k, t]).start()
        return carry

    def drain(t, carry):
        for k in range(TOP_K):
            row_copy(0, 0, 0).wait()
        return carry

    lax.fori_loop(0, tm, issue, 0)
    lax.fori_loop(0, tm, drain, 0)
    f_lo = sh_ref[:, :half]
    f_hi = sh_ref[:, half:]
    for k in range(TOP_K):
        lo, hi = _unpack_rows(_slab_load(ybuf, tm, slab, pitch, lead=(k,)))
        w = w_ref[:, k:k + 1]
        f_lo = f_lo + lo * w
        f_hi = f_hi + hi * w
    f = jnp.concatenate([f_lo, f_hi], axis=1)
    o_ref[...] = x1_ref[...] + mod_ref[5:6, :] * _rms(f, g_ref[3:4, :])


def _combine(y, idx, rank, pad_start, wts_t, sh, x1, mod_l, g4, toks):
    t, d = x1.shape
    _, pitch = _slab_pitch(d)
    tm = 128
    smem_blk = pl.BlockSpec((TOP_K, tm), lambda i, ps: (0, i), memory_space=pltpu.SMEM)
    return pl.pallas_call(
        _combine_kernel,
        grid_spec=pltpu.PrefetchScalarGridSpec(
            num_scalar_prefetch=1,
            grid=(t // tm,),
            in_specs=[
                smem_blk,
                smem_blk,
                pl.BlockSpec(memory_space=pl.ANY),
                pl.BlockSpec((tm, TOP_K), lambda i, ps: (i, 0)),
                pl.BlockSpec((tm, d), lambda i, ps: (i, 0)),
                pl.BlockSpec((tm, d), lambda i, ps: (i, 0)),
                pl.BlockSpec((None, N_MOD, d), lambda i, ps: (toks.mod_row(i, tm), 0, 0)),
                pl.BlockSpec((4, d), lambda i, ps: (0, 0)),
            ],
            out_specs=pl.BlockSpec((tm, d), lambda i, ps: (i, 0)),
            scratch_shapes=[pltpu.VMEM((TOP_K, tm * pitch, LANES), I32), pltpu.SemaphoreType.DMA((1,))],
        ),
        out_shape=jax.ShapeDtypeStruct((t, d), F32),
        compiler_params=_cparams(("arbitrary",)),
        name="moe_combine",
    )(pad_start, idx, rank, y, wts_t, sh, x1, mod_l, g4)


def _axial_tables(n_tokens, dim):
    rows = n_tokens // GRID_W
    row = jnp.repeat(jnp.arange(rows), GRID_W).astype(F32)
    col = (jnp.arange(rows * GRID_W) % GRID_W).astype(F32)
    n_freq = dim // 4
    inv = jnp.exp(-math.log(ROPE_THETA) * jnp.arange(n_freq, dtype=F32) / n_freq)
    ang = jnp.concatenate([row[:, None] * inv, col[:, None] * inv], axis=-1)
    cos, sin = jnp.cos(ang), jnp.sin(ang)
    reps = HEAD_WIDTH // dim
    cos_full = jnp.tile(jnp.concatenate([cos, cos], axis=-1), (1, reps))
    sin_full = jnp.tile(jnp.concatenate([-sin, sin], axis=-1), (1, reps))
    return cos_full, sin_full


def kernel(x_prompt, x_sample, cache_diff_k, cache_diff_v, cache_gqa_k, cache_gqa_v, state_ssd, c, c_ctx, w_ada, b_ada, norm_g, w_in, w_out, diff_lambda, diff_subnorm, gqa_qk_norm, ssd_conv_w, ssd_conv_b, ssd_dt_bias, ssd_a_log, ssd_d, ssd_norm, router_w, router_bias, exp_w_gate, exp_w_up, exp_w_down, sh_w_gate, sh_w_up, sh_w_down):
    batch, seq, d = x_prompt.shape
    dec_batch, dec_seq, _ = x_sample.shape
    depth = w_ada.shape[0]
    past = cache_diff_k.shape[2]
    n_dh = cache_diff_k.shape[3]
    n_kv = cache_gqa_k.shape[3]
    n_gh = (w_out.shape[1] - n_dh * HEAD_WIDTH - ssd_norm.shape[1]) // HEAD_WIDTH
    ssd_w = ssd_norm.shape[1]
    n_sh = ssd_w // SSD_HEAD_DIM
    bc = SSD_GROUPS * SSD_STATE
    dw, gw, kw = n_dh * HEAD_WIDTH, n_gh * HEAD_WIDTH, n_kv * HEAD_WIDTH
    assert 1 + dec_batch <= 8 and n_sh <= LANES

    t_ctx, t_lat = batch * seq, dec_batch * dec_seq
    t_all = t_ctx + t_lat
    toks = _Tokens(t_ctx, dec_seq)
    x = jnp.concatenate([x_prompt.reshape(t_ctx, d), x_sample.reshape(t_lat, d)], axis=0)

    cond = jnp.zeros((8, d), F32).at[0].set(c_ctx).at[1:1 + dec_batch].set(c)
    mod = _ada(cond, w_ada, b_ada).reshape(depth, 8, N_MOD, d)

    z_col = 3 * dw + gw + 2 * kw
    xbc_col = z_col + ssd_w
    dt_col = xbc_col + ssd_w + 2 * bc
    rope_d = _axial_tables(dec_seq, HEAD_WIDTH // 2)
    rope_g = _axial_tables(dec_seq, HEAD_WIDTH)

    tk_all = t_all * TOP_K
    n_blk = tk_all // EXPERT_ROWS + N_EXPERTS
    m_rows = n_blk * EXPERT_ROWS

    new_dk, new_dv, new_gk, new_gv, new_st = [], [], [], [], []
    for li in range(depth):
        w_in_l = w_in[li]
        w_main = w_in_l[:, :dt_col].astype(BF16)
        w_dt = jnp.zeros((d, 2 * LANES), F32)
        w_dt = w_dt.at[:, :n_sh].set(w_in_l[:, dt_col:dt_col + n_sh])
        w_dt = w_dt.at[:, LANES:LANES + n_sh].set(w_in_l[:, dt_col + n_sh:]).astype(BF16)
        proj, dt_raw = _inproj(x, mod[li], norm_g[li, 0:1], w_main, w_dt, toks)

        pad_heads = lambda v: jnp.zeros((2, 1, LANES), F32).at[:, 0, :n_sh].set(v)
        dt_bias, a_log = pad_heads(ssd_dt_bias[li]), pad_heads(ssd_a_log[li])
        d_full = jnp.repeat(ssd_d[li], SSD_HEAD_DIM)[None, :]
        norm_w = ssd_norm[li][None, :]
        lam_init = 0.8 - 0.6 * math.exp(-0.3 * li)
        diff_args = (diff_lambda[li], diff_subnorm[li][None, :], lam_init)

        qd, kd, vd, qg, kg, vg, kg_f32 = _prep(proj, gqa_qk_norm[li], 0, t_ctx, seq, n_dh, n_gh, n_kv, None)
        as_seq = lambda a, b, n: a.reshape(b, n, a.shape[-1])
        d_ctx = _attention(qd, [(as_seq(kd, batch, seq), as_seq(vd, batch, seq), seq, False)], n_dh, 1, batch, seq,
                           diff=diff_args, hps=n_dh)
        g_ctx = _attention(qg, [(as_seq(kg, batch, seq), as_seq(vg, batch, seq), seq, False)], n_gh, n_gh // n_kv,
                           batch, seq, hps=n_gh)
        xbc_c = _conv_silu(proj, xbc_col, ssd_w + 2 * bc, ssd_conv_w[li], ssd_conv_b[li][None, :], 0, batch, seq)
        y_c, h_last = _ssd_scan(xbc_c, dt_raw, dt_bias, a_log, None, 0, batch, seq, n_sh, True)
        s_ctx = _ssd_gate(y_c, xbc_c, proj, z_col, 0, d_full, norm_w, ssd_w)
        new_dk.append(proj[:t_ctx, dw:2 * dw].reshape(batch, seq, n_dh, 2, HEAD_WIDTH // 2))
        new_dv.append(proj[:t_ctx, 2 * dw:3 * dw].reshape(batch, seq, n_dh, HEAD_WIDTH))
        new_gk.append(kg_f32.reshape(batch, seq, n_kv, HEAD_WIDTH))
        new_gv.append(proj[:t_ctx, 3 * dw + gw + kw: 3 * dw + gw + 2 * kw].reshape(batch, seq, n_kv, HEAD_WIDTH))
        new_st.append(h_last.reshape(batch, 2, n_sh, SSD_HEAD_DIM, SSD_STATE))

        qd, kd, vd, qg, kg, vg = _prep(proj, gqa_qk_norm[li], t_ctx, t_lat, dec_seq, n_dh, n_gh, n_kv,
                                       rope_d + rope_g)
        c_dk = cache_diff_k[:, li].reshape(dec_batch, past, dw)
        c_dv = cache_diff_v[:, li].reshape(dec_batch, past, dw)
        c_gk = cache_gqa_k[:, li].reshape(dec_batch, past, kw)
        c_gv = cache_gqa_v[:, li].reshape(dec_batch, past, kw)
        d_lat = _attention(qd, [(c_dk, c_dv, past, False), (as_seq(kd, dec_batch, dec_seq), vd, dec_seq, True)],
                           n_dh, 1, dec_batch, dec_seq, diff=diff_args)
        g_lat = _attention(qg, [(c_gk, c_gv, past, False), (as_seq(kg, dec_batch, dec_seq), vg, dec_seq, True)],
                           n_gh, n_gh // n_kv, dec_batch, dec_seq)
        xbc_l = _conv_silu(proj, xbc_col, ssd_w + 2 * bc, ssd_conv_w[li], ssd_conv_b[li][None, :], t_ctx,
                           dec_batch, dec_seq)
        h0 = state_ssd[:, li].reshape(dec_batch, 2, ssd_w, SSD_STATE)
        (y_l,) = _ssd_scan(xbc_l, dt_raw, dt_bias, a_log, h0, t_ctx, dec_batch, dec_seq, n_sh, False)
        s_lat = _ssd_gate(y_l, xbc_l, proj, z_col, t_ctx, d_full, norm_w, ssd_w)

        cat = lambda a, b: jnp.concatenate([a, b], axis=0)
        u = _outproj(cat(d_ctx, d_lat), cat(g_ctx, g_lat), cat(s_ctx, s_lat), w_out[li].astype(BF16))
        x1, h2p, idx, wts, rank, counts = _router(u, x, mod[li], norm_g[li], router_w[li].T.astype(BF16),
                                                 router_bias[li][:, None], toks)

        counts = counts[:, 0].astype(I32)
        padded = (counts + EXPERT_ROWS - 1) // EXPERT_ROWS * EXPERT_ROWS
        pad_end = jnp.cumsum(padded)
        pad_start = (pad_end - padded).astype(I32)
        fill_start = jnp.where(counts > 0, pad_end - EXPERT_ROWS, -1).astype(I32)
        n_used = (pad_end[-1:] // EXPERT_ROWS).astype(I32)
        blk_row0 = jnp.arange(n_blk, dtype=I32) * EXPERT_ROWS
        blk_e = jnp.minimum(jnp.sum((pad_end[None, :] <= blk_row0[:, None]).astype(I32), axis=1), N_EXPERTS - 1)

        hs = _dispatch(h2p, idx, rank, pad_start, fill_start, m_rows, _slab_pitch(d)[1])
        y = _experts(hs, blk_e, n_used, _cast_layer_bf16(exp_w_gate, li), _cast_layer_bf16(exp_w_up, li),
                     _cast_layer_bf16(exp_w_down, li))
        sh = _shared_expert(h2p, sh_w_gate[li].astype(BF16), sh_w_up[li].astype(BF16), sh_w_down[li].astype(BF16))
        x = _combine(y, idx, rank, pad_start, wts.T, sh, x1, mod[li], norm_g[li], toks)

    y_prompt = x[:t_ctx].reshape(batch, seq, d)
    y_sample = x[t_ctx:].reshape(dec_batch, dec_seq, d)
    stack = lambda parts: jnp.stack(parts, axis=1)
    return (y_prompt, y_sample, stack(new_dk), stack(new_dv), stack(new_gk), stack(new_gv), stack(new_st))
```

```python
import functools
import math

import jax
import jax.numpy as jnp
from jax import lax
from jax.experimental import pallas as pl
from jax.experimental.pallas import tpu as pltpu

F32 = jnp.float32
BF16 = jnp.bfloat16
I32 = jnp.int32

LANES = 128
HEAD_WIDTH = 128
GRID_W = 64
GQA_KV_HEADS = 2
SSD_HEAD_DIM = 64
SSD_GROUPS = 4
SSD_STATE = 128
SSD_CHUNK = 128
ROPE_THETA = 10000.0
RMS_EPS = 1e-6
N_MOD = 6
N_EXPERTS = 64
TOP_K = 8
N_EXPERT_GROUPS = 8
TOPK_GROUPS = 4
ROUTED_SCALE = 2.5
EXPERT_ROWS = 256
ATTN_TQ = 512
ATTN_TK = 2048
VMEM_LIMIT = 56 * 1024 * 1024
NEG_INF = float("-inf")
LOG2E = 1.4426950408889634


def _cparams(sem):
    return pltpu.CompilerParams(dimension_semantics=sem, vmem_limit_bytes=VMEM_LIMIT)


def _pick(n, cands):
    for c in cands:
        if n % c == 0:
            return c
    raise ValueError(f"no tile for {n} in {cands}")


def _silu(x):
    return x * (1.0 / (1.0 + jnp.exp(-x)))


def _rms(x, g):
    return x * lax.rsqrt(jnp.mean(x * x, axis=-1, keepdims=True) + RMS_EPS) * g


def _nt_dot(a, b):
    return lax.dot_general(a, b, (((1,), (1,)), ((), ())), preferred_element_type=F32)


def _tn_dot(a, b):
    return lax.dot_general(a, b, (((0,), (0,)), ((), ())), preferred_element_type=F32)


HI_MASK = -65536


def _slab_pitch(d):
    slab = d // (2 * LANES)
    assert slab % 8 == 0
    return slab, (slab if (slab // 8) % 2 == 1 else slab + 8)


def _pack_rows(v):
    half = v.shape[1] // 2
    lo = lax.bitcast_convert_type(v[:, :half].astype(BF16).astype(F32), I32)
    hi = lax.bitcast_convert_type(v[:, half:].astype(BF16).astype(F32), I32)
    return lax.shift_right_logical(lo, 16) | (hi & HI_MASK)


def _unpack_rows(w):
    lo = lax.bitcast_convert_type(lax.shift_left(w, 16), F32)
    hi = lax.bitcast_convert_type(w & HI_MASK, F32)
    return lo, hi


def _slab_load(ref, n, slab, pitch, lead=()):
    return jnp.concatenate([ref[lead + (pl.ds(a, n, stride=pitch), slice(None))] for a in range(slab)], axis=1)


def _slab_store(ref, words, slab, pitch):
    n = words.shape[0]
    for a in range(slab):
        ref[pl.ds(a, n, stride=pitch), :] = words[:, a * LANES:(a + 1) * LANES]
    for a in range(slab, pitch):
        ref[pl.ds(a, n, stride=pitch), :] = jnp.zeros((n, LANES), I32)


def _ada_kernel(cond_ref, w_ref, b_ref, o_ref):
    s = _silu(cond_ref[...]).astype(BF16)
    o_ref[0] = jnp.dot(s, w_ref[0].astype(BF16), preferred_element_type=F32) + b_ref[0]


def _ada(cond, w_ada, b_ada):
    depth, d, n = w_ada.shape
    tn = _pick(n, (512, 256, 128))
    return pl.pallas_call(
        _ada_kernel,
        grid=(depth, n // tn),
        in_specs=[
            pl.BlockSpec((8, d), lambda l, j: (0, 0)),
            pl.BlockSpec((1, d, tn), lambda l, j: (l, 0, j)),
            pl.BlockSpec((1, 1, tn), lambda l, j: (l, 0, j)),
        ],
        out_specs=pl.BlockSpec((1, 8, tn), lambda l, j: (l, 0, j)),
        out_shape=jax.ShapeDtypeStruct((depth, 8, n), F32),
        compiler_params=_cparams(("arbitrary", "arbitrary")),
        name="ada_mod",
    )(cond, w_ada, b_ada.reshape(depth, 1, n))


class _Tokens:
    def __init__(self, n_ctx, n_lat_seq):
        self.n_ctx = n_ctx
        self.n_lat_seq = n_lat_seq

    def mod_row(self, i, tm):
        assert self.n_ctx % tm == 0 and self.n_lat_seq % tm == 0
        ctx_tiles = self.n_ctx // tm
        per_b = self.n_lat_seq // tm
        return jnp.where(i < ctx_tiles, 0, 1 + (i - ctx_tiles) // per_b)


def _inproj_kernel(x_ref, mod_ref, g_ref, w_ref, wdt_ref, o_ref, odt_ref, h_scr):
    @pl.when(pl.program_id(1) == 0)
    def _():
        h = _rms(x_ref[...], g_ref[...]) * (1.0 + mod_ref[1:2, :]) + mod_ref[0:1, :]
        hb = h.astype(BF16)
        h_scr[...] = hb
        odt_ref[...] = jnp.dot(hb, wdt_ref[...], preferred_element_type=F32)

    o_ref[...] = jnp.dot(h_scr[...], w_ref[...], preferred_element_type=F32)


def _inproj(x, mod_l, g0, w_main, w_dt, toks):
    t, d = x.shape
    n = w_main.shape[1]
    tm = _pick(t, (512, 256, 128))
    tm = min(tm, _pick(toks.n_ctx, (512, 256, 128)), _pick(toks.n_lat_seq, (512, 256, 128)))
    tn = _pick(n, (512, 256, 128))
    ndt = w_dt.shape[1]
    return pl.pallas_call(
        _inproj_kernel,
        grid=(t // tm, n // tn),
        in_specs=[
            pl.BlockSpec((tm, d), lambda i, j: (i, 0)),
            pl.BlockSpec((None, N_MOD, d), lambda i, j: (toks.mod_row(i, tm), 0, 0)),
            pl.BlockSpec((1, d), lambda i, j: (0, 0)),
            pl.BlockSpec((d, tn), lambda i, j: (0, j)),
            pl.BlockSpec((d, ndt), lambda i, j: (0, 0)),
        ],
        out_specs=[
            pl.BlockSpec((tm, tn), lambda i, j: (i, j)),
            pl.BlockSpec((tm, ndt), lambda i, j: (i, 0)),
        ],
        out_shape=[jax.ShapeDtypeStruct((t, n), F32), jax.ShapeDtypeStruct((t, ndt), F32)],
        scratch_shapes=[pltpu.VMEM((tm, d), BF16)],
        compiler_params=_cparams(("arbitrary", "arbitrary")),
        name="in_proj",
    )(x, mod_l, g0, w_main, w_dt)


def _attn_tq(seq):
    return min(ATTN_TQ, seq)


def _attn_tk(seq):
    return min(ATTN_TK, seq)


def _rope128(x, cos, sin, half):
    if half == 64:
        partner = pltpu.roll(x, 64, 1)
    else:
        lane = lax.broadcasted_iota(I32, x.shape, 1)
        partner = jnp.where((lane & (2 * half - 1)) < half, pltpu.roll(x, LANES - half, 1), pltpu.roll(x, half, 1))
    return x * cos + partner * sin


def _prep_kernel(*refs, n_dh, n_gh, n_kv, rope):
    if rope:
        p_ref, qkn_ref, cd_ref, sd_ref, cg_ref, sg_ref, qd_ref, kd_ref, vd_ref, qg_ref, kg_ref, vg_ref = refs
    else:
        p_ref, qkn_ref, qd_ref, kd_ref, vd_ref, qg_ref, kg_ref, vg_ref, kgf_ref = refs
    dw = n_dh * HEAD_WIDTH
    gw = n_gh * HEAD_WIDTH
    kw = n_kv * HEAD_WIDTH
    tn = p_ref.shape[0]
    d_scale = (HEAD_WIDTH // 2) ** -0.5 * LOG2E
    g_scale = HEAD_WIDTH ** -0.5 * LOG2E

    def put_v(v_ref, h, v):
        sl = slice(h * HEAD_WIDTH, (h + 1) * HEAD_WIDTH)
        if rope:
            v_ref[sl, :] = v.T.astype(BF16)
        else:
            v_ref[:, sl] = v.astype(BF16)

    for h in range(n_dh):
        sl = slice(h * HEAD_WIDTH, (h + 1) * HEAD_WIDTH)
        q = p_ref[:, sl]
        k = p_ref[:, dw + h * HEAD_WIDTH: dw + (h + 1) * HEAD_WIDTH]
        if rope:
            q = _rope128(q, cd_ref[...], sd_ref[...], HEAD_WIDTH // 4)
            k = _rope128(k, cd_ref[...], sd_ref[...], HEAD_WIDTH // 4)
        qd_ref[sl, :] = (q * d_scale).T.astype(BF16)
        kd_ref[:, sl] = k.astype(BF16)
        put_v(vd_ref, h, p_ref[:, 2 * dw + h * HEAD_WIDTH: 2 * dw + (h + 1) * HEAD_WIDTH])
    off = 3 * dw
    for h in range(n_gh):
        sl = slice(h * HEAD_WIDTH, (h + 1) * HEAD_WIDTH)
        q = _rms(p_ref[:, off + h * HEAD_WIDTH: off + (h + 1) * HEAD_WIDTH], qkn_ref[0:1, :])
        if rope:
            q = _rope128(q, cg_ref[...], sg_ref[...], HEAD_WIDTH // 2)
        qg_ref[sl, :] = (q * g_scale).T.astype(BF16)
    off += gw
    for h in range(n_kv):
        sl = slice(h * HEAD_WIDTH, (h + 1) * HEAD_WIDTH)
        k = _rms(p_ref[:, off + h * HEAD_WIDTH: off + (h + 1) * HEAD_WIDTH], qkn_ref[1:2, :])
        if rope:
            k = _rope128(k, cg_ref[...], sg_ref[...], HEAD_WIDTH // 2)
        else:
            kgf_ref[:, sl] = k
        kg_ref[:, sl] = k.astype(BF16)
        put_v(vg_ref, h, p_ref[:, off + kw + h * HEAD_WIDTH: off + kw + (h + 1) * HEAD_WIDTH])


def _prep(proj, qk_norm, row0, n_rows, seq, n_dh, n_gh, n_kv, rope_tabs):
    dw, gw, kw = n_dh * HEAD_WIDTH, n_gh * HEAD_WIDTH, n_kv * HEAD_WIDTH
    width = 3 * dw + gw + 2 * kw
    tn = _attn_tq(seq)
    tk = _attn_tk(seq)
    assert row0 % tn == 0 and seq % tk == 0 and tk % tn == 0
    r0 = row0 // tn
    n_tiles = n_rows // tn
    rope = rope_tabs is not None
    in_specs = [
        pl.BlockSpec((tn, width), lambda i: (r0 + i, 0)),
        pl.BlockSpec((2, HEAD_WIDTH), lambda i: (0, 0)),
    ]
    args = [proj, qk_norm]
    if rope:
        per_seq = seq // tn
        for tab in rope_tabs:
            in_specs.append(pl.BlockSpec((tn, HEAD_WIDTH), lambda i: (i % per_seq, 0)))
            args.append(tab)
    tok = lambda w: (pl.BlockSpec((tn, w), lambda i: (i, 0)), jax.ShapeDtypeStruct((n_rows, w), BF16))
    q_t = lambda w: (pl.BlockSpec((None, w, tn), lambda i: (i, 0, 0)), jax.ShapeDtypeStruct((n_tiles, w, tn), BF16))
    tpc = tk // tn
    v_t = lambda w: (pl.BlockSpec((None, w, tn), lambda i: (i // tpc, 0, i % tpc)),
                     jax.ShapeDtypeStruct((n_rows // tk, w, tk), BF16))
    v_out = v_t if rope else tok
    outs = [q_t(dw), tok(dw), v_out(dw), q_t(gw), tok(kw), v_out(kw)]
    if not rope:
        outs.append((pl.BlockSpec((tn, kw), lambda i: (i, 0)), jax.ShapeDtypeStruct((n_rows, kw), F32)))
    return pl.pallas_call(
        functools.partial(_prep_kernel, n_dh=n_dh, n_gh=n_gh, n_kv=n_kv, rope=rope),
        grid=(n_tiles,),
        in_specs=in_specs,
        out_specs=[o[0] for o in outs],
        out_shape=[o[1] for o in outs],
        compiler_params=_cparams(("arbitrary",)),
        name="attn_prep_rope" if rope else "attn_prep",
    )(*args)


def _softmax_step(state, s, pv):
    m, l, acc = state
    m_new = jnp.maximum(m, jnp.max(s, axis=0, keepdims=True))
    alpha = jnp.exp2(m - m_new)
    p = jnp.exp2(s - m_new)
    l = alpha * l + jnp.sum(p, axis=0, keepdims=True)
    acc = alpha * acc + pv(p.astype(BF16))
    return m_new, l, acc


def _attn_kernel(*refs, segs, diff, lam_init, hps, kv_group):
    n_seg = len(segs)
    q_ref = refs[0]
    kv_refs = refs[1: 1 + 2 * n_seg]
    s_scr = p_scr = None
    if any(v_t for _, _, v_t in segs):
        refs, (s_scr, p_scr) = refs[:-2], refs[-2:]
    if diff:
        lam_ref, sub_ref, o_ref = refs[1 + 2 * n_seg:]
    else:
        lam_ref = sub_ref = None
        (o_ref,) = refs[1 + 2 * n_seg:]
    for hh in range(hps):
        kvs = slice((hh // kv_group) * HEAD_WIDTH, (hh // kv_group + 1) * HEAD_WIDTH) if hps > 1 else slice(None)
        hs = slice(hh * HEAD_WIDTH, (hh + 1) * HEAD_WIDTH)
        _attn_head(q_ref, kv_refs, lam_ref, sub_ref, o_ref, s_scr, p_scr, hs, kvs, segs, diff, lam_init)


def _attn_head(q_ref, kv_refs, lam_ref, sub_ref, o_ref, s_scr, p_scr, hs, kvs, segs, diff, lam_init):
    qt = q_ref[hs, :]
    tq = qt.shape[1]
    if diff:
        row = lax.broadcasted_iota(I32, qt.shape, 0)
        qs = [jnp.where(row < HEAD_WIDTH // 2, qt, jnp.zeros_like(qt)),
              jnp.where(row < HEAD_WIDTH // 2, jnp.zeros_like(qt), qt)]
    else:
        qs = [qt]
    init = (jnp.full((1, tq), NEG_INF, F32), jnp.zeros((1, tq), F32), jnp.zeros((HEAD_WIDTH, tq), F32))
    states = tuple(init for _ in qs)

    for si, (t_len, tk, v_t) in enumerate(segs):
        k_ref, v_ref = kv_refs[2 * si], kv_refs[2 * si + 1]

        def chunk(c, sts, k_ref=k_ref, v_ref=v_ref, tk=tk, v_t=v_t):
            start = pl.multiple_of(c * tk, tk)
            k = k_ref[pl.ds(start, tk), kvs].astype(BF16)
            if v_t:
                vt = v_ref[c, kvs, :]
                pv = lambda p: jnp.dot(vt, p, preferred_element_type=F32)
            else:
                v = v_ref[pl.ds(start, tk), kvs].astype(BF16)
                pv = lambda p: _tn_dot(v, p)
            return tuple(_softmax_step(st, jnp.dot(k, qm, preferred_element_type=F32), pv)
                         for st, qm in zip(sts, qs))

        n_chunks = t_len // tk
        if n_chunks == 1:
            states = chunk(0, states)
        elif not v_t:
            states = lax.fori_loop(0, n_chunks, chunk, states)
        else:
            for c in range(n_chunks):
                slot = c % 2
                k = k_ref[c * tk:(c + 1) * tk, kvs]
                for i, qm in enumerate(qs):
                    s_scr[slot, i] = jnp.dot(k, qm, preferred_element_type=F32)
                new_states = []
                for i, (m, l, acc) in enumerate(states):
                    m_new = jnp.maximum(m, jnp.max(s_scr[slot, i], axis=0, keepdims=True))
                    alpha = jnp.exp2(m - m_new)
                    p = jnp.exp2(s_scr[slot, i] - m_new)
                    l = alpha * l + jnp.sum(p, axis=0, keepdims=True)
                    p_scr[slot, i] = p.astype(BF16)
                    acc = alpha * acc + jnp.dot(v_ref[c, kvs, :], p_scr[slot, i], preferred_element_type=F32)
                    new_states.append((m_new, l, acc))
                states = tuple(new_states)

    outs = [acc / l for (_, l, acc) in states]
    if diff:
        lp = lam_ref[...]
        lam = (jnp.exp(jnp.sum(lp[0:1] * lp[1:2], axis=-1, keepdims=True))
               - jnp.exp(jnp.sum(lp[2:3] * lp[3:4], axis=-1, keepdims=True)) + lam_init)
        o = outs[0] - lam * outs[1]
        o = o * lax.rsqrt(jnp.mean(o * o, axis=0, keepdims=True) + RMS_EPS) * (1.0 - lam_init)
        o = o.T * sub_ref[...]
    else:
        o = outs[0].T
    o_ref[:, hs] = o.astype(o_ref.dtype)


def _attention(qt, kvs, n_heads, kv_group, batch, seq, diff=None, hps=1):
    tq = _attn_tq(seq)
    per_seq = seq // tq
    assert hps == 1 or (hps % kv_group == 0 and n_heads % hps == 0)
    qw = hps * HEAD_WIDTH
    kvw = HEAD_WIDTH if hps == 1 else (hps // kv_group) * HEAD_WIDTH
    kv_blk = (lambda h: h // kv_group) if hps == 1 else (lambda h: h)
    in_specs = [pl.BlockSpec((None, qw, tq), lambda b, h, i: (b * per_seq + i, h, 0))]
    args = [qt]
    segs = []
    tk_t = 0
    for k, v, t_len, v_t in kvs:
        in_specs.append(pl.BlockSpec((None, t_len, kvw), lambda b, h, i: (b, 0, kv_blk(h))))
        if v_t:
            tk = tk_t = _attn_tk(t_len)
            in_specs.append(pl.BlockSpec((t_len // tk, kvw, tk), lambda b, h, i: (b, kv_blk(h), 0)))
        else:
            tk = _pick(t_len, (256, 128))
            in_specs.append(pl.BlockSpec((None, t_len, kvw), lambda b, h, i: (b, 0, kv_blk(h))))
        args += [k, v]
        segs.append((t_len, tk, v_t))
    lam_init = 0.0
    if diff is not None:
        lam_params, subnorm, lam_init = diff
        in_specs.append(pl.BlockSpec(lam_params.shape, lambda b, h, i: (0, 0)))
        in_specs.append(pl.BlockSpec((1, HEAD_WIDTH), lambda b, h, i: (0, 0)))
        args += [lam_params, subnorm]
    scratch = []
    if tk_t:
        n_maps = 2 if diff is not None else 1
        scratch = [pltpu.VMEM((2, n_maps, tk_t, tq), F32), pltpu.VMEM((2, n_maps, tk_t, tq), BF16)]
    return pl.pallas_call(
        functools.partial(_attn_kernel, segs=tuple(segs), diff=diff is not None, lam_init=lam_init,
                          hps=hps, kv_group=kv_group),
        grid=(batch, n_heads // hps, per_seq),
        scratch_shapes=scratch,
        in_specs=in_specs,
        out_specs=pl.BlockSpec((tq, qw), lambda b, h, i: (b * per_seq + i, h)),
        out_shape=jax.ShapeDtypeStruct((batch * seq, n_heads * HEAD_WIDTH), BF16),
        compiler_params=_cparams(("arbitrary", "arbitrary", "arbitrary")),
        name="diff_attn" if diff is not None else "gqa_attn",
    )(*args)


def _conv_kernel(u_ref, w_ref, b_ref, o_ref):
    u = u_ref[...]
    n = u.shape[0]
    row = lax.broadcasted_iota(I32, u.shape, 0)
    prev = jnp.where(row == 0, 0.0, pltpu.roll(u, 1, 0))
    nxt = jnp.where(row == n - 1, 0.0, pltpu.roll(u, n - 1, 0))
    y = w_ref[0:1, :] * prev + w_ref[1:2, :] * u + w_ref[2:3, :] * nxt + b_ref[...]
    o_ref[...] = _silu(y)


def _conv_silu(proj, col0, width, conv_w, conv_b, row0, batch, seq):
    tc = 256
    assert col0 % tc == 0 and width % tc == 0 and row0 % seq == 0
    c0, r0 = col0 // tc, row0 // seq
    return pl.pallas_call(
        _conv_kernel,
        grid=(batch, width // tc),
        in_specs=[
            pl.BlockSpec((seq, tc), lambda b, j: (r0 + b, c0 + j)),
            pl.BlockSpec((conv_w.shape[0], tc), lambda b, j: (0, j)),
            pl.BlockSpec((1, tc), lambda b, j: (0, j)),
        ],
        out_specs=pl.BlockSpec((seq, tc), lambda b, j: (b, j)),
        out_shape=jax.ShapeDtypeStruct((batch * seq, width), F32),
        compiler_params=_cparams(("arbitrary", "arbitrary")),
        name="ssd_conv",
    )(proj, conv_w, conv_b)


def _split3(x):
    hi = x.astype(BF16)
    r = x - hi.astype(F32)
    mid = r.astype(BF16)
    lo = (r - mid.astype(F32)).astype(BF16)
    return hi, mid, lo


def _ssd_kernel(*refs, n_heads, heads_per_group, has_h0, want_state):
    it = iter(refs)
    xs_ref, bm_ref, cm_ref, dt_ref, dtb_ref, alog_ref = (next(it) for _ in range(6))
    h0_ref = next(it) if has_h0 else None
    y_ref = next(it)
    hl_ref = next(it) if want_state else None
    state = next(it)

    fwd = pl.program_id(1) == 0
    c = pl.program_id(2)
    L = SSD_CHUNK

    @pl.when(c == 0)
    def _():
        if has_h0:
            state[...] = h0_ref[...]
        else:
            state[...] = jnp.zeros_like(state)

    z = dt_ref[...] + dtb_ref[...]
    dt = jnp.maximum(z, 0.0) + jnp.log1p(jnp.exp(-jnp.abs(z)))
    dta = dt * (-jnp.exp(alog_ref[...]))
    qi = lax.broadcasted_iota(I32, (L, L), 0)
    ki = lax.broadcasted_iota(I32, (L, L), 1)
    ahead = jnp.where(fwd, qi - ki, ki - qi)
    causal = ahead >= 0
    causal_t = ahead <= 0
    tri = causal.astype(F32).astype(BF16)
    tri_t = causal_t.astype(F32).astype(BF16)
    a_cum = sum(jnp.dot(tri, p, preferred_element_type=F32) for p in _split3(dta))
    a_cum_t = sum(jnp.dot(p, tri_t, preferred_element_type=F32) for p in _split3(dta.T))
    dt_t = dt.T
    total = jnp.sum(dta, axis=0, keepdims=True)
    lane_lo = lax.broadcasted_iota(I32, (L, HEAD_WIDTH), 1) < SSD_HEAD_DIM
    row_lo = lax.broadcasted_iota(I32, (HEAD_WIDTH, SSD_STATE), 0) < SSD_HEAD_DIM

    def spread(x):
        rows = lax.broadcasted_iota(I32, (LANES, n_heads * SSD_HEAD_DIM), 0)
        cols = lax.broadcasted_iota(I32, (LANES, n_heads * SSD_HEAD_DIM), 1)
        pick = (lax.shift_right_logical(cols, SSD_HEAD_DIM.bit_length() - 1) == rows).astype(F32).astype(BF16)
        return sum(jnp.dot(p, pick, preferred_element_type=F32) for p in _split3(x)[:2])

    ea_x = spread(jnp.exp(a_cum))
    te_x = spread(jnp.exp(total - a_cum) * dt)

    n_groups = n_heads // heads_per_group
    for g in range(n_groups):
        gs = slice(g * SSD_STATE, (g + 1) * SSD_STATE)
        b_g = bm_ref[:, gs].astype(BF16)
        c_g = cm_ref[:, gs].astype(BF16)
        cb = _nt_dot(c_g, b_g)
        for j in range(heads_per_group // 2):
            pair = (g * heads_per_group) // 2 + j
            ps = slice(pair * HEAD_WIDTH, (pair + 1) * HEAD_WIDTH)
            x_pair = xs_ref[:, ps]
            ws, tots = [], []
            for hh in (2 * pair, 2 * pair + 1):
                seg = a_cum[:, hh:hh + 1] - a_cum_t[hh:hh + 1, :]
                decay = jnp.exp(jnp.where(causal, seg, NEG_INF))
                ws.append(cb * decay * dt_t[hh:hh + 1, :])
                tots.append(total[:, hh:hh + 1])
            w2 = jnp.concatenate(ws, axis=1).astype(BF16)
            x2 = jnp.concatenate([jnp.where(lane_lo, x_pair, 0.0), jnp.where(lane_lo, 0.0, x_pair)],
                                 axis=0).astype(BF16)
            y_diag = jnp.dot(w2, x2, preferred_element_type=F32)
            h_in = state[ps, :]
            y_off = _nt_dot(c_g, h_in.astype(BF16)) * ea_x[:, ps]
            y_ref[:, ps] = y_diag + y_off
            xw = x_pair * te_x[:, ps]
            st = _tn_dot(xw.astype(BF16), b_g)
            dec = jnp.where(row_lo, jnp.exp(tots[0]), jnp.exp(tots[1]))
            state[ps, :] = h_in * dec + st

    if want_state:
        @pl.when(c == pl.num_programs(2) - 1)
        def _():
            hl_ref[...] = state[...]


def _ssd_scan(xbc, dt_raw, dt_bias, a_log, h0, row0_dt, batch, seq, n_heads, want_state):
    L = SSD_CHUNK
    nc = seq // L
    width = n_heads * SSD_HEAD_DIM
    n_groups = SSD_GROUPS
    bc = n_groups * SSD_STATE
    assert width % bc == 0 and row0_dt % L == 0
    r0 = row0_dt // L

    def rb(b, d, c):
        return b * nc + c + d * (nc - 1 - 2 * c)

    in_specs = [
        pl.BlockSpec((L, width), lambda b, d, c: (rb(b, d, c), 0)),
        pl.BlockSpec((L, bc), lambda b, d, c: (rb(b, d, c), width // bc)),
        pl.BlockSpec((L, bc), lambda b, d, c: (rb(b, d, c), width // bc + 1)),
        pl.BlockSpec((L, LANES), lambda b, d, c: (r0 + rb(b, d, c), d)),
        pl.BlockSpec((None, 1, LANES), lambda b, d, c: (d, 0, 0)),
        pl.BlockSpec((None, 1, LANES), lambda b, d, c: (d, 0, 0)),
    ]
    args = [xbc, xbc, xbc, dt_raw, dt_bias, a_log]
    has_h0 = h0 is not None
    if has_h0:
        in_specs.append(pl.BlockSpec((None, None, width, SSD_STATE), lambda b, d, c: (b, d, 0, 0)))
        args.append(h0)
    out_specs = [pl.BlockSpec((None, L, width), lambda b, d, c: (d, rb(b, d, c), 0))]
    out_shape = [jax.ShapeDtypeStruct((2, batch * seq, width), F32)]
    if want_state:
        out_specs.append(pl.BlockSpec((None, None, width, SSD_STATE), lambda b, d, c: (b, d, 0, 0)))
        out_shape.append(jax.ShapeDtypeStruct((batch, 2, width, SSD_STATE), F32))
    return pl.pallas_call(
        functools.partial(_ssd_kernel, n_heads=n_heads, heads_per_group=n_heads // n_groups,
                          has_h0=has_h0, want_state=want_state),
        grid=(batch, 2, nc),
        in_specs=in_specs,
        out_specs=out_specs,
        out_shape=out_shape,
        scratch_shapes=[pltpu.VMEM((width, SSD_STATE), F32)],
        compiler_params=_cparams(("arbitrary", "arbitrary", "arbitrary")),
        name="ssd_scan",
    )(*args)


def _ssd_gate_kernel(y_ref, xs_ref, z_ref, d_ref, nw_ref, o_ref):
    y = y_ref[0] + y_ref[1] + d_ref[...] * xs_ref[...]
    y = y * _silu(z_ref[...])
    o_ref[...] = _rms(y, nw_ref[...]).astype(o_ref.dtype)


def _ssd_gate(y2, xbc, proj, z_col0, row0, d_full, norm_w, width):
    n_rows = y2.shape[1]
    gw = width // SSD_GROUPS
    tm = _pick(n_rows, (512, 256, 128))
    assert z_col0 % gw == 0 and row0 % tm == 0
    zc, r0 = z_col0 // gw, row0 // tm
    return pl.pallas_call(
        _ssd_gate_kernel,
        grid=(n_rows // tm, SSD_GROUPS),
        in_specs=[
            pl.BlockSpec((2, tm, gw), lambda i, g: (0, i, g)),
            pl.BlockSpec((tm, gw), lambda i, g: (i, g)),
            pl.BlockSpec((tm, gw), lambda i, g: (r0 + i, zc + g)),
            pl.BlockSpec((1, gw), lambda i, g: (0, g)),
            pl.BlockSpec((1, gw), lambda i, g: (0, g)),
        ],
        out_specs=pl.BlockSpec((tm, gw), lambda i, g: (i, g)),
        out_shape=jax.ShapeDtypeStruct((n_rows, width), BF16),
        compiler_params=_cparams(("arbitrary", "arbitrary")),
        name="ssd_gate",
    )(y2, xbc, proj, d_full, norm_w)


def _outproj_kernel(d_ref, g_ref, s_ref, wd_ref, wg_ref, ws_ref, o_ref):
    acc = jnp.dot(d_ref[...], wd_ref[...], preferred_element_type=F32)
    acc += jnp.dot(g_ref[...], wg_ref[...], preferred_element_type=F32)
    acc += jnp.dot(s_ref[...], ws_ref[...], preferred_element_type=F32)
    o_ref[...] = acc


def _outproj(d_out, g_out, s_out, w_out):
    t = d_out.shape[0]
    dw, gw, sw = d_out.shape[1], g_out.shape[1], s_out.shape[1]
    d = w_out.shape[1]
    assert dw == gw and sw % dw == 0 and (dw + gw) % sw == 0
    tm = _pick(t, (512, 256, 128))
    tn = _pick(d, (1024, 512, 256, 128))
    return pl.pallas_call(
        _outproj_kernel,
        grid=(t // tm, d // tn),
        in_specs=[
            pl.BlockSpec((tm, dw), lambda i, j: (i, 0)),
            pl.BlockSpec((tm, gw), lambda i, j: (i, 0)),
            pl.BlockSpec((tm, sw), lambda i, j: (i, 0)),
            pl.BlockSpec((dw, tn), lambda i, j: (0, j)),
            pl.BlockSpec((gw, tn), lambda i, j: (1, j)),
            pl.BlockSpec((sw, tn), lambda i, j: ((dw + gw) // sw, j)),
        ],
        out_specs=pl.BlockSpec((tm, tn), lambda i, j: (i, j)),
        out_shape=jax.ShapeDtypeStruct((t, d), F32),
        compiler_params=_cparams(("arbitrary", "arbitrary")),
        name="out_proj",
    )(d_out, g_out, s_out, w_out, w_out, w_out)


def _first_index(hit, idx, sentinel):
    return jnp.min(jnp.where(hit, idx, sentinel), axis=0, keepdims=True)


def _router_kernel(u_ref, x_ref, mod_ref, g_ref, rw_ref, rb_ref,
                   x1_ref, h2_ref, idx_ref, wts_ref, rank_ref, cnt_ref, carry):
    per_group = N_EXPERTS // N_EXPERT_GROUPS

    @pl.when(pl.program_id(0) == 0)
    def _():
        carry[...] = jnp.zeros_like(carry)

    x1 = x_ref[...] + mod_ref[2:3, :] * _rms(u_ref[...], g_ref[1:2, :])
    x1_ref[...] = x1
    h2 = _rms(x1, g_ref[2:3, :]) * (1.0 + mod_ref[4:5, :]) + mod_ref[3:4, :]
    slab, pitch = _slab_pitch(h2.shape[1])
    _slab_store(h2_ref, _pack_rows(h2), slab, pitch)
    logits =_nt_dot(rw_ref[...], h2.astype(BF16))
    scores = 1.0 / (1.0 + jnp.exp(-logits))
    sel = scores + rb_ref[...]
    tm = sel.shape[1]
    sub = lax.broadcasted_iota(I32, (per_group, tm), 0)

    sel_g = [sel[g * per_group:(g + 1) * per_group, :] for g in range(N_EXPERT_GROUPS)]
    sc_g = [scores[g * per_group:(g + 1) * per_group, :] for g in range(N_EXPERT_GROUPS)]
    gscore = jnp.zeros((N_EXPERT_GROUPS, tm), F32)
    gsub = lax.broadcasted_iota(I32, (N_EXPERT_GROUPS, tm), 0)
    for g in range(N_EXPERT_GROUPS):
        v = sel_g[g]
        m1 = jnp.max(v, axis=0, keepdims=True)
        i1 = _first_index(v == m1, sub, per_group)
        m2 = jnp.max(jnp.where(sub == i1, NEG_INF, v), axis=0, keepdims=True)
        gscore = jnp.where(gsub == g, m1 + m2, gscore)
    gmask = jnp.zeros((N_EXPERT_GROUPS, tm), F32)
    gv = gscore
    for _ in range(TOPK_GROUPS):
        m = jnp.max(gv, axis=0, keepdims=True)
        hit = gsub == _first_index(gv == m, gsub, N_EXPERT_GROUPS)
        gmask = jnp.where(hit, 1.0, gmask)
        gv = jnp.where(hit, NEG_INF, gv)
    vals = [jnp.where(gmask[g:g + 1, :] > 0.0, sel_g[g], NEG_INF) for g in range(N_EXPERT_GROUPS)]
    eid = [sub + g * per_group for g in range(N_EXPERT_GROUPS)]

    picked = [jnp.zeros((per_group, tm), F32) for _ in range(N_EXPERT_GROUPS)]
    w_rows, idx_rows = [], []
    for _ in range(TOP_K):
        m = functools.reduce(jnp.maximum, [jnp.max(v, axis=0, keepdims=True) for v in vals])
        first = functools.reduce(jnp.minimum,
                                 [_first_index(v == m, e, N_EXPERTS) for v, e in zip(vals, eid)])
        w = jnp.zeros((1, tm), F32)
        for g in range(N_EXPERT_GROUPS):
            hit = eid[g] == first
            w = w + jnp.sum(jnp.where(hit, sc_g[g], 0.0), axis=0, keepdims=True)
            vals[g] = jnp.where(hit, NEG_INF, vals[g])
            picked[g] = jnp.where(hit, 1.0, picked[g])
        w_rows.append(w)
        idx_rows.append(first)
    wsum = functools.reduce(lambda a, b: a + b, w_rows)

    onehot = jnp.concatenate(picked, axis=0)
    ti = lax.broadcasted_iota(I32, (tm, tm), 0)
    tj = lax.broadcasted_iota(I32, (tm, tm), 1)
    upper = (ti < tj).astype(F32).astype(BF16)
    rank = jnp.dot(onehot.astype(BF16), upper, preferred_element_type=F32) + carry[:, 0:1]
    new_carry = carry[...] + jnp.sum(onehot, axis=1, keepdims=True)
    carry[...] = new_carry
    cnt_ref[...] = new_carry
    for k in range(TOP_K):
        idx_ref[k:k + 1, :] = idx_rows[k]
        wts_ref[k:k + 1, :] = w_rows[k] / wsum * ROUTED_SCALE
        r = jnp.zeros((1, tm), F32)
        for g in range(N_EXPERT_GROUPS):
            r = r + jnp.sum(jnp.where(eid[g] == idx_rows[k], rank[g * per_group:(g + 1) * per_group, :], 0.0),
                            axis=0, keepdims=True)
        rank_ref[k:k + 1, :] = r.astype(I32)


def _router(u, x, mod_l, g4, rw_t, rbias, toks):
    t, d = x.shape
    tm = min(256, _pick(toks.n_ctx, (256, 128)), _pick(toks.n_lat_seq, (256, 128)))
    _, pitch = _slab_pitch(d)
    row = lambda i: (i, 0)
    col = lambda i: (0, i)
    fixed = lambda i: (0, 0)
    return pl.pallas_call(
        _router_kernel,
        grid=(t // tm,),
        in_specs=[
            pl.BlockSpec((tm, d), row),
            pl.BlockSpec((tm, d), row),
            pl.BlockSpec((None, N_MOD, d), lambda i: (toks.mod_row(i, tm), 0, 0)),
            pl.BlockSpec((4, d), fixed),
            pl.BlockSpec((N_EXPERTS, d), fixed),
            pl.BlockSpec((N_EXPERTS, 1), fixed),
        ],
        out_specs=[
            pl.BlockSpec((tm, d), row),
            pl.BlockSpec((tm * pitch, LANES), row),
            pl.BlockSpec((TOP_K, tm), col),
            pl.BlockSpec((TOP_K, tm), col),
            pl.BlockSpec((TOP_K, tm), col),
            pl.BlockSpec((N_EXPERTS, LANES), fixed),
        ],
        out_shape=[
            jax.ShapeDtypeStruct((t, d), F32),
            jax.ShapeDtypeStruct((t * pitch, LANES), I32),
            jax.ShapeDtypeStruct((TOP_K, t), I32),
            jax.ShapeDtypeStruct((TOP_K, t), F32),
            jax.ShapeDtypeStruct((TOP_K, t), I32),
            jax.ShapeDtypeStruct((N_EXPERTS, LANES), F32),
        ],
        scratch_shapes=[pltpu.VMEM((N_EXPERTS, LANES), F32)],
        compiler_params=_cparams(("arbitrary",)),
        name="router",
    )(u, x, mod_l, g4, rw_t, rbias)


def _dispatch_kernel(pstart_ref, fill_ref, idx_ref, rank_ref, h_ref, hs_ref, zbuf, sem, *, pitch):
    tchunk = idx_ref.shape[1]
    blk = zbuf.shape[0]

    def row_copy(t, d):
        src = h_ref.at[pl.ds(pl.multiple_of(t * pitch, pitch), pitch)]
        dst = hs_ref.at[pl.ds(pl.multiple_of(d * pitch, pitch), pitch)]
        return pltpu.make_async_copy(src, dst, sem.at[0])

    @pl.when(pl.program_id(0) == 0)
    def _():
        zbuf[...] = jnp.zeros_like(zbuf)

        def fill_copy(e):
            start = pl.multiple_of(fill_ref[e] * pitch, blk)
            return pltpu.make_async_copy(zbuf, hs_ref.at[pl.ds(start, blk)], sem.at[1])

        def start(e, carry):
            @pl.when(fill_ref[e] >= 0)
            def _():
                fill_copy(e).start()
            return carry

        def wait(e, carry):
            @pl.when(fill_ref[e] >= 0)
            def _():
                fill_copy(e).wait()
            return carry

        lax.fori_loop(0, N_EXPERTS, start, 0)
        lax.fori_loop(0, N_EXPERTS, wait, 0)

    def issue(t, carry):
        for k in range(TOP_K):
            row_copy(t, pstart_ref[idx_ref[k, t]] + rank_ref[k, t]).start(priority=k % 2)
        return carry

    def drain(t, carry):
        for k in range(TOP_K):
            row_copy(0, 0).wait()
        return carry

    lax.fori_loop(0, tchunk, issue, 0)
    lax.fori_loop(0, tchunk, drain, 0)


def _dispatch(h2p, idx, rank, pad_start, fill_start, m_rows, pitch):
    t = h2p.shape[0] // pitch
    tchunk = _pick(t, (512, 256, 128))
    smem_blk = pl.BlockSpec((TOP_K, tchunk), lambda i, ps, fs: (0, i), memory_space=pltpu.SMEM)
    return pl.pallas_call(
        functools.partial(_dispatch_kernel, pitch=pitch),
        grid_spec=pltpu.PrefetchScalarGridSpec(
            num_scalar_prefetch=2,
            grid=(t // tchunk,),
            in_specs=[
                smem_blk,
                smem_blk,
                pl.BlockSpec((tchunk * pitch, LANES), lambda i, ps, fs: (i, 0)),
            ],
            out_specs=pl.BlockSpec(memory_space=pl.ANY),
            scratch_shapes=[pltpu.VMEM((EXPERT_ROWS * pitch, LANES), I32), pltpu.SemaphoreType.DMA((2,))],
        ),
        out_shape=jax.ShapeDtypeStruct((m_rows * pitch, LANES), I32),
        compiler_params=_cparams(("arbitrary",)),
        name="moe_dispatch",
    )(pad_start, fill_start, idx, rank, h2p)


def _cast_kernel(w_ref, o_ref):
    o_ref[...] = w_ref[...].astype(BF16)


def _cast_layer_bf16(w, li):
    _, e, a, b = w.shape
    return pl.pallas_call(
        _cast_kernel,
        grid=(e,),
        in_specs=[pl.BlockSpec((None, None, a, b), lambda i: (li, i, 0, 0))],
        out_specs=pl.BlockSpec((None, a, b), lambda i: (i, 0, 0)),
        out_shape=jax.ShapeDtypeStruct((e, a, b), BF16),
        compiler_params=_cparams(("arbitrary",)),
        name="cast_bf16",
    )(w)


def _swiglu_packed(h_ref, n, wg_ref, wu_ref, wd_ref):
    d = wg_ref.shape[0]
    slab, pitch = _slab_pitch(d)
    lo, hi = _unpack_rows(_slab_load(h_ref, n, slab, pitch))
    lo, hi = lo.astype(BF16), hi.astype(BF16)
    half = d // 2
    a = (jnp.dot(lo, wg_ref[:half, :], preferred_element_type=F32)
         + jnp.dot(hi, wg_ref[half:, :], preferred_element_type=F32))
    b = (jnp.dot(lo, wu_ref[:half, :], preferred_element_type=F32)
         + jnp.dot(hi, wu_ref[half:, :], preferred_element_type=F32))
    return jnp.dot((_silu(a) * b).astype(BF16), wd_ref[...], preferred_element_type=F32)


def _expert_kernel(blk_e_ref, n_used_ref, hs_ref, wg_ref, wu_ref, wd_ref, y_ref):
    @pl.when(pl.program_id(0) < n_used_ref[0])
    def _():
        y = _swiglu_packed(hs_ref, EXPERT_ROWS, wg_ref, wu_ref, wd_ref)
        slab, pitch = _slab_pitch(y.shape[1])
        _slab_store(y_ref, _pack_rows(y), slab, pitch)


def _experts(hs, blk_e, n_used, wg, wu, wd):
    d, ff = wg.shape[1], wg.shape[2]
    _, pitch = _slab_pitch(d)
    m_rows = hs.shape[0] // pitch
    n_blk = m_rows // EXPERT_ROWS

    def blk(i, blk_e, n_used):
        return jnp.minimum(i, n_used[0] - 1)

    return pl.pallas_call(
        _expert_kernel,
        grid_spec=pltpu.PrefetchScalarGridSpec(
            num_scalar_prefetch=2,
            grid=(n_blk,),
            in_specs=[
                pl.BlockSpec((EXPERT_ROWS * pitch, LANES), lambda i, be, nu: (blk(i, be, nu), 0)),
                pl.BlockSpec((None, d, ff), lambda i, be, nu: (be[blk(i, be, nu)], 0, 0)),
                pl.BlockSpec((None, d, ff), lambda i, be, nu: (be[blk(i, be, nu)], 0, 0)),
                pl.BlockSpec((None, ff, d), lambda i, be, nu: (be[blk(i, be, nu)], 0, 0)),
            ],
            out_specs=pl.BlockSpec((EXPERT_ROWS * pitch, LANES), lambda i, be, nu: (blk(i, be, nu), 0)),
        ),
        out_shape=jax.ShapeDtypeStruct((m_rows * pitch, LANES), I32),
        compiler_params=_cparams(("arbitrary",)),
        name="moe_experts",
    )(blk_e, n_used, hs, wg, wu, wd)


def _shared_kernel(h_ref, wg_ref, wu_ref, wd_ref, o_ref):
    o_ref[...] = _swiglu_packed(h_ref, o_ref.shape[0], wg_ref, wu_ref, wd_ref)


def _shared_expert(h2p, wg, wu, wd):
    d, ff = wg.shape
    _, pitch = _slab_pitch(d)
    t = h2p.shape[0] // pitch
    tm = _pick(t, (256, 128))
    fixed = lambda i: (0, 0)
    return pl.pallas_call(
        _shared_kernel,
        grid=(t // tm,),
        in_specs=[
            pl.BlockSpec((tm * pitch, LANES), lambda i: (i, 0)),
            pl.BlockSpec((d, ff), fixed),
            pl.BlockSpec((d, ff), fixed),
            pl.BlockSpec((ff, d), fixed),
        ],
        out_specs=pl.BlockSpec((tm, d), lambda i: (i, 0)),
        out_shape=jax.ShapeDtypeStruct((t, d), F32),
        compiler_params=_cparams(("arbitrary",)),
        name="shared_expert",
    )(h2p, wg, wu, wd)


def _combine_kernel(pstart_ref, idx_ref, rank_ref, y_ref, w_ref, sh_ref, x1_ref, mod_ref, g_ref, o_ref, ybuf, sem):
    tm, d_model = x1_ref.shape
    slab, pitch = _slab_pitch(d_model)
    half = d_model // 2

    def row_copy(k, t, d):
        src = y_ref.at[pl.ds(pl.multiple_of(d * pitch, pitch), pitch)]
        dst = ybuf.at[k, pl.ds(pl.multiple_of(t * pitch, pitch), pitch)]
        return pltpu.make_async_copy(src, dst, sem.at[0])

    def issue(t, carry):
        for k in range(TOP_K):
            row_copy(k, t, pstart_ref[idx_ref[k, t]] + rank_ref[k, t]).start(priority=k % 2)
        return carry

    def drain(t, carry):
        for k in range(TOP_K):
            row_copy(0, 0, 0).wait()
        return carry

    lax.fori_loop(0, tm, issue, 0)
    lax.fori_loop(0, tm, drain, 0)
    f_lo = sh_ref[:, :half]
    f_hi = sh_ref[:, half:]
    for k in range(TOP_K):
        lo, hi = _unpack_rows(_slab_load(ybuf, tm, slab, pitch, lead=(k,)))
        w = w_ref[:, k:k + 1]
        f_lo = f_lo + lo * w
        f_hi = f_hi + hi * w
    f = jnp.concatenate([f_lo, f_hi], axis=1)
    o_ref[...] = x1_ref[...] + mod_ref[5:6, :] * _rms(f, g_ref[3:4, :])


def _combine(y, idx, rank, pad_start, wts_t, sh, x1, mod_l, g4, toks):
    t, d = x1.shape
    _, pitch = _slab_pitch(d)
    tm = 128
    smem_blk = pl.BlockSpec((TOP_K, tm), lambda i, ps: (0, i), memory_space=pltpu.SMEM)
    return pl.pallas_call(
        _combine_kernel,
        grid_spec=pltpu.PrefetchScalarGridSpec(
            num_scalar_prefetch=1,
            grid=(t // tm,),
            in_specs=[
                smem_blk,
                smem_blk,
                pl.BlockSpec(memory_space=pl.ANY),
                pl.BlockSpec((tm, TOP_K), lambda i, ps: (i, 0)),
                pl.BlockSpec((tm, d), lambda i, ps: (i, 0)),
                pl.BlockSpec((tm, d), lambda i, ps: (i, 0)),
                pl.BlockSpec((None, N_MOD, d), lambda i, ps: (toks.mod_row(i, tm), 0, 0)),
                pl.BlockSpec((4, d), lambda i, ps: (0, 0)),
            ],
            out_specs=pl.BlockSpec((tm, d), lambda i, ps: (i, 0)),
            scratch_shapes=[pltpu.VMEM((TOP_K, tm * pitch, LANES), I32), pltpu.SemaphoreType.DMA((1,))],
        ),
        out_shape=jax.ShapeDtypeStruct((t, d), F32),
        compiler_params=_cparams(("arbitrary",)),
        name="moe_combine",
    )(pad_start, idx, rank, y, wts_t, sh, x1, mod_l, g4)


def _axial_tables(n_tokens, dim):
    rows = n_tokens // GRID_W
    row = jnp.repeat(jnp.arange(rows), GRID_W).astype(F32)
    col = (jnp.arange(rows * GRID_W) % GRID_W).astype(F32)
    n_freq = dim // 4
    inv = jnp.exp(-math.log(ROPE_THETA) * jnp.arange(n_freq, dtype=F32) / n_freq)
    ang = jnp.concatenate([row[:, None] * inv, col[:, None] * inv], axis=-1)
    cos, sin = jnp.cos(ang), jnp.sin(ang)
    reps = HEAD_WIDTH // dim
    cos_full = jnp.tile(jnp.concatenate([cos, cos], axis=-1), (1, reps))
    sin_full = jnp.tile(jnp.concatenate([-sin, sin], axis=-1), (1, reps))
    return cos_full, sin_full


def kernel(x_prompt, x_sample, cache_diff_k, cache_diff_v, cache_gqa_k, cache_gqa_v, state_ssd, c, c_ctx, w_ada, b_ada, norm_g, w_in, w_out, diff_lambda, diff_subnorm, gqa_qk_norm, ssd_conv_w, ssd_conv_b, ssd_dt_bias, ssd_a_log, ssd_d, ssd_norm, router_w, router_bias, exp_w_gate, exp_w_up, exp_w_down, sh_w_gate, sh_w_up, sh_w_down):
    batch, seq, d = x_prompt.shape
    dec_batch, dec_seq, _ = x_sample.shape
    depth = w_ada.shape[0]
    past = cache_diff_k.shape[2]
    n_dh = cache_diff_k.shape[3]
    n_kv = cache_gqa_k.shape[3]
    n_gh = (w_out.shape[1] - n_dh * HEAD_WIDTH - ssd_norm.shape[1]) // HEAD_WIDTH
    ssd_w = ssd_norm.shape[1]
    n_sh = ssd_w // SSD_HEAD_DIM
    bc = SSD_GROUPS * SSD_STATE
    dw, gw, kw = n_dh * HEAD_WIDTH, n_gh * HEAD_WIDTH, n_kv * HEAD_WIDTH
    assert 1 + dec_batch <= 8 and n_sh <= LANES

    t_ctx, t_lat = batch * seq, dec_batch * dec_seq
    t_all = t_ctx + t_lat
    toks = _Tokens(t_ctx, dec_seq)
    x = jnp.concatenate([x_prompt.reshape(t_ctx, d), x_sample.reshape(t_lat, d)], axis=0)

    cond = jnp.zeros((8, d), F32).at[0].set(c_ctx).at[1:1 + dec_batch].set(c)
    mod = _ada(cond, w_ada, b_ada).reshape(depth, 8, N_MOD, d)

    z_col = 3 * dw + gw + 2 * kw
    xbc_col = z_col + ssd_w
    dt_col = xbc_col + ssd_w + 2 * bc
    rope_d = _axial_tables(dec_seq, HEAD_WIDTH // 2)
    rope_g = _axial_tables(dec_seq, HEAD_WIDTH)

    tk_all = t_all * TOP_K
    n_blk = tk_all // EXPERT_ROWS + N_EXPERTS
    m_rows = n_blk * EXPERT_ROWS

    new_dk, new_dv, new_gk, new_gv, new_st = [], [], [], [], []
    for li in range(depth):
        w_in_l = w_in[li]
        w_main = w_in_l[:, :dt_col].astype(BF16)
        w_dt = jnp.zeros((d, 2 * LANES), F32)
        w_dt = w_dt.at[:, :n_sh].set(w_in_l[:, dt_col:dt_col + n_sh])
        w_dt = w_dt.at[:, LANES:LANES + n_sh].set(w_in_l[:, dt_col + n_sh:]).astype(BF16)
        proj, dt_raw = _inproj(x, mod[li], norm_g[li, 0:1], w_main, w_dt, toks)

        pad_heads = lambda v: jnp.zeros((2, 1, LANES), F32).at[:, 0, :n_sh].set(v)
        dt_bias, a_log = pad_heads(ssd_dt_bias[li]), pad_heads(ssd_a_log[li])
        d_full = jnp.repeat(ssd_d[li], SSD_HEAD_DIM)[None, :]
        norm_w = ssd_norm[li][None, :]
        lam_init = 0.8 - 0.6 * math.exp(-0.3 * li)
        diff_args = (diff_lambda[li], diff_subnorm[li][None, :], lam_init)

        qd, kd, vd, qg, kg, vg, kg_f32 = _prep(proj, gqa_qk_norm[li], 0, t_ctx, seq, n_dh, n_gh, n_kv, None)
        as_seq = lambda a, b, n: a.reshape(b, n, a.shape[-1])
        d_ctx = _attention(qd, [(as_seq(kd, batch, seq), as_seq(vd, batch, seq), seq, False)], n_dh, 1, batch, seq,
                           diff=diff_args, hps=n_dh)
        g_ctx = _attention(qg, [(as_seq(kg, batch, seq), as_seq(vg, batch, seq), seq, False)], n_gh, n_gh // n_kv,
                           batch, seq, hps=n_gh)
        xbc_c = _conv_silu(proj, xbc_col, ssd_w + 2 * bc, ssd_conv_w[li], ssd_conv_b[li][None, :], 0, batch, seq)
        y_c, h_last = _ssd_scan(xbc_c, dt_raw, dt_bias, a_log, None, 0, batch, seq, n_sh, True)
        s_ctx = _ssd_gate(y_c, xbc_c, proj, z_col, 0, d_full, norm_w, ssd_w)
        new_dk.append(proj[:t_ctx, dw:2 * dw].reshape(batch, seq, n_dh, 2, HEAD_WIDTH // 2))
        new_dv.append(proj[:t_ctx, 2 * dw:3 * dw].reshape(batch, seq, n_dh, HEAD_WIDTH))
        new_gk.append(kg_f32.reshape(batch, seq, n_kv, HEAD_WIDTH))
        new_gv.append(proj[:t_ctx, 3 * dw + gw + kw: 3 * dw + gw + 2 * kw].reshape(batch, seq, n_kv, HEAD_WIDTH))
        new_st.append(h_last.reshape(batch, 2, n_sh, SSD_HEAD_DIM, SSD_STATE))

        qd, kd, vd, qg, kg, vg = _prep(proj, gqa_qk_norm[li], t_ctx, t_lat, dec_seq, n_dh, n_gh, n_kv,
                                       rope_d + rope_g)
        c_dk = cache_diff_k[:, li].reshape(dec_batch, past, dw)
        c_dv = cache_diff_v[:, li].reshape(dec_batch, past, dw)
        c_gk = cache_gqa_k[:, li].reshape(dec_batch, past, kw)
        c_gv = cache_gqa_v[:, li].reshape(dec_batch, past, kw)
        d_lat = _attention(qd, [(c_dk, c_dv, past, False), (as_seq(kd, dec_batch, dec_seq), vd, dec_seq, True)],
                           n_dh, 1, dec_batch, dec_seq, diff=diff_args)
        g_lat = _attention(qg, [(c_gk, c_gv, past, False), (as_seq(kg, dec_batch, dec_seq), vg, dec_seq, True)],
                           n_gh, n_gh // n_kv, dec_batch, dec_seq)
        xbc_l = _conv_silu(proj, xbc_col, ssd_w + 2 * bc, ssd_conv_w[li], ssd_conv_b[li][None, :], t_ctx,
                           dec_batch, dec_seq)
        h0 = state_ssd[:, li].reshape(dec_batch, 2, ssd_w, SSD_STATE)
        (y_l,) = _ssd_scan(xbc_l, dt_raw, dt_bias, a_log, h0, t_ctx, dec_batch, dec_seq, n_sh, False)
        s_lat = _ssd_gate(y_l, xbc_l, proj, z_col, t_ctx, d_full, norm_w, ssd_w)

        cat = lambda a, b: jnp.concatenate([a, b], axis=0)
        u = _outproj(cat(d_ctx, d_lat), cat(g_ctx, g_lat), cat(s_ctx, s_lat), w_out[li].astype(BF16))
        x1, h2p, idx, wts, rank, counts = _router(u, x, mod[li], norm_g[li], router_w[li].T.astype(BF16),
                                                 router_bias[li][:, None], toks)

        counts = counts[:, 0].astype(I32)
        padded = (counts + EXPERT_ROWS - 1) // EXPERT_ROWS * EXPERT_ROWS
        pad_end = jnp.cumsum(padded)
        pad_start = (pad_end - padded).astype(I32)
        fill_start = jnp.where(counts > 0, pad_end - EXPERT_ROWS, -1).astype(I32)
        n_used = (pad_end[-1:] // EXPERT_ROWS).astype(I32)
        blk_row0 = jnp.arange(n_blk, dtype=I32) * EXPERT_ROWS
        blk_e = jnp.minimum(jnp.sum((pad_end[None, :] <= blk_row0[:, None]).astype(I32), axis=1), N_EXPERTS - 1)

        hs = _dispatch(h2p, idx, rank, pad_start, fill_start, m_rows, _slab_pitch(d)[1])
        y = _experts(hs, blk_e, n_used, _cast_layer_bf16(exp_w_gate, li), _cast_layer_bf16(exp_w_up, li),
                     _cast_layer_bf16(exp_w_down, li))
        sh = _shared_expert(h2p, sh_w_gate[li].astype(BF16), sh_w_up[li].astype(BF16), sh_w_down[li].astype(BF16))
        x = _combine(y, idx, rank, pad_start, wts.T, sh, x1, mod[li], norm_g[li], toks)

    y_prompt = x[:t_ctx].reshape(batch, seq, d)
    y_sample = x[t_ctx:].reshape(dec_batch, dec_seq, d)
    stack = lambda parts: jnp.stack(parts, axis=1)
    return (y_prompt, y_sample, stack(new_dk), stack(new_dv), stack(new_gk), stack(new_gv), stack(new_st))
```
